```python
import jax, jax.numpy as jnp
from jax import lax
import numpy as np

D_MODEL = 1024
BATCH = 4
SEQ = 4096
DEPTH = 1

SB_HEADS = 8
SB_HEAD_DIM = 64
SB_WIDTH = SB_HEADS * SB_HEAD_DIM
Q_BLOCK = 128
CONV_GROUPS = 8
CONV_GROUP_DIM = 64
CONV_WIDTH = CONV_GROUPS * CONV_GROUP_DIM
CONV_KSIZE = 3
N_BRANCHES = 2
N_GROUPS = 4
EXPERTS_PER_GROUP = 8
N_EXPERTS = N_GROUPS * EXPERTS_PER_GROUP
TOP_K_IN_GROUP = 2
D_EXPERT = 256
NORM_EPS = 1e-6
IN_SIZES = (SB_WIDTH, SB_WIDTH, SB_WIDTH, CONV_WIDTH, CONV_WIDTH, CONV_WIDTH, D_MODEL, D_MODEL)
IN_PROJ_WIDTH = sum(IN_SIZES)
IN_SPLITS = tuple(int(s) for s in np.cumsum(IN_SIZES)[:-1])

kernel_name = "hybrid_stickbreak_shortconv_hmoe"


def rms_norm(x, g):
    xf = x.astype(jnp.float32)
    y = xf * lax.rsqrt(jnp.mean(xf * xf, axis=-1, keepdims=True) + NORM_EPS)
    return (y * g.astype(jnp.float32)).astype(x.dtype)


def stick_breaking_attention(q, k, v):
    B, H, S, dh = q.shape
    n_blocks = S // Q_BLOCK
    scale = SB_HEAD_DIM ** -0.5
    kf = k.astype(jnp.float32)
    vf = v.astype(jnp.float32)
    key_pos = jnp.arange(S)

    def one_block(i):
        start = i * Q_BLOCK
        qb = lax.dynamic_slice_in_dim(q, start, Q_BLOCK, axis=2).astype(jnp.float32)
        z = jnp.einsum('bhqd,bhkd->bhqk', qb, kf) * scale
        q_pos = start + jnp.arange(Q_BLOCK)
        mask = key_pos[None, :] < q_pos[:, None]
        neg_log_1m_beta = jnp.where(mask, jax.nn.softplus(z), 0.0)
        tail = lax.cumsum(neg_log_1m_beta, axis=3, reverse=True) - neg_log_1m_beta
        a = jnp.where(mask, jnp.exp(jax.nn.log_sigmoid(z) - tail), 0.0)
        return jnp.einsum('bhqk,bhkd->bhqd', a, vf)

    out = lax.map(one_block, jnp.arange(n_blocks))
    out = jnp.transpose(out, (1, 0, 3, 2, 4)).reshape(B, S, H * dh)
    return out.astype(q.dtype)


def short_conv_mixer(b_gate, c_gate, u, conv_w):
    S = u.shape[1]
    cu = c_gate * u
    cu_pad = jnp.pad(cu, ((0, 0), (CONV_KSIZE - 1, 0), (0, 0)))
    y = conv_w[0] * cu_pad[:, 0:S]
    for j in range(1, CONV_KSIZE):
        y = y + conv_w[j] * cu_pad[:, j:j + S]
    return b_gate * y


def hierarchical_moe(h, w_rg, b_rg, w_re, b_re, w_gate, w_up, w_down):
    N = h.shape[0]
    hf = h.astype(jnp.float32)
    g_logits = hf @ w_rg.astype(jnp.float32) + b_rg.astype(jnp.float32)
    g_prob = jax.nn.softmax(g_logits, axis=-1)
    _, g_idx = lax.top_k(g_logits, 1)
    g_w = jnp.take_along_axis(g_prob, g_idx, axis=-1)
    e_logits = (hf @ w_re.astype(jnp.float32) + b_re.astype(jnp.float32)).reshape(N, N_GROUPS, EXPERTS_PER_GROUP)
    e_in_group = jnp.take_along_axis(e_logits, g_idx[:, :, None], axis=1)[:, 0]
    top_v, top_i = lax.top_k(e_in_group, TOP_K_IN_GROUP)
    top_w = jax.nn.softmax(top_v, axis=-1)
    within = jnp.sum(jax.nn.one_hot(top_i, EXPERTS_PER_GROUP, dtype=jnp.float32) * top_w[..., None], axis=1)
    combine = (jax.nn.one_hot(g_idx[:, 0], N_GROUPS, dtype=jnp.float32)[:, :, None]
               * within[:, None, :] * g_w[:, :, None]).astype(h.dtype)
    y = jnp.zeros_like(h)
    for g in range(N_GROUPS):
        sl = slice(g * EXPERTS_PER_GROUP, (g + 1) * EXPERTS_PER_GROUP)
        act = jax.nn.silu(jnp.einsum('nd,edf->nef', h, w_gate[sl])) * jnp.einsum('nd,edf->nef', h, w_up[sl])
        act = act * combine[:, g, :, None]
        y = y + jnp.einsum('nef,efd->nd', act, w_down[sl])
    return y


def setup_inputs(seed: int = 0) -> dict:
    key = jax.random.key(seed)
    ks = jax.random.split(key, 20)
    f32 = jnp.float32
    nrm = lambda k, shape, fan_in: jax.random.normal(k, shape, f32) * (fan_in ** -0.5)
    gain = lambda k, shape: 1.0 + 0.05 * jax.random.normal(k, shape, f32)
    return {
        "x": jax.random.normal(ks[0], (BATCH, SEQ, D_MODEL), f32),
        "norm_mix_g": gain(ks[1], (DEPTH, D_MODEL)),
        "w_in": nrm(ks[2], (DEPTH, D_MODEL, IN_PROJ_WIDTH), D_MODEL),
        "q_norm_g": gain(ks[3], (DEPTH, SB_HEAD_DIM)),
        "k_norm_g": gain(ks[4], (DEPTH, SB_HEAD_DIM)),
        "conv_w": nrm(ks[5], (DEPTH, CONV_KSIZE, CONV_WIDTH), CONV_KSIZE),
        "w_sb_branch": nrm(ks[6], (DEPTH, SB_WIDTH, D_MODEL), SB_WIDTH),
        "w_conv_branch": nrm(ks[7], (DEPTH, CONV_WIDTH, D_MODEL), CONV_WIDTH),
        "w_out": nrm(ks[8], (DEPTH, D_MODEL, D_MODEL), D_MODEL),
        "norm_ffn_g": gain(ks[9], (DEPTH, D_MODEL)),
        "w_router_group": nrm(ks[10], (DEPTH, D_MODEL, N_GROUPS), D_MODEL),
        "b_router_group": 0.01 * jax.random.normal(ks[11], (DEPTH, N_GROUPS), f32),
        "w_router_expert": nrm(ks[12], (DEPTH, D_MODEL, N_EXPERTS), D_MODEL),
        "b_router_expert": 0.01 * jax.random.normal(ks[13], (DEPTH, N_EXPERTS), f32),
        "w_gate_e": nrm(ks[14], (DEPTH, N_EXPERTS, D_MODEL, D_EXPERT), D_MODEL),
        "w_up_e": nrm(ks[15], (DEPTH, N_EXPERTS, D_MODEL, D_EXPERT), D_MODEL),
        "w_down_e": nrm(ks[16], (DEPTH, N_EXPERTS, D_EXPERT, D_MODEL), D_EXPERT),
    }


def reference(x, norm_mix_g, w_in, q_norm_g, k_norm_g, conv_w, w_sb_branch, w_conv_branch,
              w_out, norm_ffn_g, w_router_group, b_router_group, w_router_expert,
              b_router_expert, w_gate_e, w_up_e, w_down_e):
    B, S, D = x.shape
    for l in range(DEPTH):
        h = rms_norm(x, norm_mix_g[l])
        proj = h @ w_in[l]
        q, k, v, c_b, c_c, c_u, g_a, g_b = jnp.split(proj, IN_SPLITS, axis=-1)
        q = rms_norm(q.reshape(B, S, SB_HEADS, SB_HEAD_DIM), q_norm_g[l]).transpose(0, 2, 1, 3)
        k = rms_norm(k.reshape(B, S, SB_HEADS, SB_HEAD_DIM), k_norm_g[l]).transpose(0, 2, 1, 3)
        v = v.reshape(B, S, SB_HEADS, SB_HEAD_DIM).transpose(0, 2, 1, 3)
        branch_a = stick_breaking_attention(q, k, v) @ w_sb_branch[l]
        branch_b = short_conv_mixer(c_b, c_c, c_u, conv_w[l]) @ w_conv_branch[l]
        merged = jax.nn.sigmoid(g_a) * branch_a + jax.nn.sigmoid(g_b) * branch_b
        x = x + merged @ w_out[l]
        h2 = rms_norm(x, norm_ffn_g[l]).reshape(B * S, D)
        y = hierarchical_moe(h2, w_router_group[l], b_router_group[l], w_router_expert[l],
                             b_router_expert[l], w_gate_e[l], w_up_e[l], w_down_e[l])
        x = x + y.reshape(B, S, D)
    return x
```

```python
import functools

import jax
import jax.numpy as jnp
from jax import lax
from jax.experimental import pallas as pl
from jax.experimental.pallas import tpu as pltpu

F32 = jnp.float32
BF16 = jnp.bfloat16

D_MODEL = 1024
N_HEADS = 8
HEAD_DIM = 64
WIDTH = N_HEADS * HEAD_DIM
CONV_K = 3
N_GROUPS = 4
E_PER_GROUP = 8
N_EXPERTS = N_GROUPS * E_PER_GROUP
D_EXPERT = 256
EPS = 1e-6

LANES = 128
SUBLANES = 8
HEADS_PER_STEP = LANES // HEAD_DIM

TM_PROJ = 256
TQ = 256
TK = 256
TM_MOE = 256
ROUTE_ROWS = 48
VMEM_LIMIT = 48 * 1024 * 1024

_NT = (((1,), (1,)), ((), ()))


def _dot(a, b):
    return jnp.dot(a, b, preferred_element_type=F32)


def _split_bf16(v):
    hi = v.astype(BF16)
    lo = (v - hi.astype(F32)).astype(BF16)
    return hi, lo


def _inproj_kernel(x_ref, g1_ref, w_ref, qg_ref, kg_ref, pm_ref, cw_ref,
                   q_ref, k_ref, v_ref, y_ref, sa_ref, sb_ref, cu_scr, *, tiles_per_seq):
    tm = x_ref.shape[0]
    x = x_ref[...]
    ms = jnp.mean(x * x, axis=-1, keepdims=True)
    h = (x * lax.rsqrt(ms + EPS) * g1_ref[...]).astype(BF16)

    def proj(lo, width):
        return _dot(h, w_ref[:, lo:lo + width])

    def head_norm(t, g_ref):
        hms = _dot((t * t).astype(BF16), pm_ref[...])
        return (t * lax.rsqrt(hms + EPS) * g_ref[...]).astype(BF16)

    q_ref[...] = head_norm(proj(0, WIDTH), qg_ref)
    k_ref[...] = head_norm(proj(WIDTH, WIDTH), kg_ref)
    v_ref[...] = proj(2 * WIDTH, WIDTH).astype(BF16)

    c_b = proj(3 * WIDTH, WIDTH)
    cu = proj(4 * WIDTH, WIDTH) * proj(5 * WIDTH, WIDTH)

    @pl.when(pl.program_id(0) % tiles_per_seq == 0)
    def _():
        cu_scr[0:SUBLANES, :] = jnp.zeros((SUBLANES, WIDTH), F32)

    cu_scr[SUBLANES:SUBLANES + tm, :] = cu
    cw = cw_ref[...]
    conv = (cw[0:1, :] * cu_scr[SUBLANES - 2:SUBLANES - 2 + tm, :]
            + cw[1:2, :] * cu_scr[SUBLANES - 1:SUBLANES - 1 + tm, :]
            + cw[2:3, :] * cu)
    y_ref[...] = (c_b * conv).astype(BF16)
    cu_scr[0:SUBLANES, :] = cu_scr[tm:tm + SUBLANES, :]

    sa_ref[...] = jax.nn.sigmoid(proj(6 * WIDTH, D_MODEL))
    sb_ref[...] = jax.nn.sigmoid(proj(6 * WIDTH + D_MODEL, D_MODEL))


def _inproj_call(x2d, g1, w_in, qg, kg, pm, conv_w, seq_len):
    n = x2d.shape[0]
    tm = TM_PROJ
    in_w = w_in.shape[1]
    const = lambda shape: pl.BlockSpec(shape, lambda i: (0, 0))
    rows = lambda width: pl.BlockSpec((tm, width), lambda i: (i, 0))
    return pl.pallas_call(
        functools.partial(_inproj_kernel, tiles_per_seq=seq_len // tm),
        grid=(n // tm,),
        in_specs=[rows(D_MODEL), const((1, D_MODEL)), const((D_MODEL, in_w)), const((1, WIDTH)),
                  const((1, WIDTH)), const((WIDTH, WIDTH)), const((CONV_K, WIDTH))],
        out_specs=[rows(WIDTH), rows(WIDTH), rows(WIDTH), rows(WIDTH), rows(D_MODEL), rows(D_MODEL)],
        out_shape=[jax.ShapeDtypeStruct((n, WIDTH), BF16)] * 4
                  + [jax.ShapeDtypeStruct((n, D_MODEL), F32)] * 2,
        scratch_shapes=[pltpu.VMEM((SUBLANES + tm, WIDTH), F32)],
        compiler_params=pltpu.CompilerParams(dimension_semantics=("arbitrary",),
                                             vmem_limit_bytes=VMEM_LIMIT),
        name="inproj",
    )(x2d, g1, w_in, qg, kg, pm, conv_w)


def _softplus(z):
    return jnp.maximum(z, 0.0) + jnp.log(1.0 + jnp.exp(-jnp.abs(z)))


def _attn_kernel(q_ref, k_ref, v_ref, tri_ref, o_ref, acc_scr, c_scr):
    i = pl.program_id(2)
    tq = q_ref.shape[0]
    q2 = q_ref[...]
    lane = lax.broadcasted_iota(jnp.int32, q2.shape, 1)
    zero = jnp.zeros_like(q2)
    q_heads = (jnp.where(lane < HEAD_DIM, q2, zero), jnp.where(lane < HEAD_DIM, zero, q2))
    tri = tri_ref[...]

    def visit(j, mask):
        start = pl.multiple_of(j * TK, TK)
        kj = k_ref[pl.ds(start, TK), :]
        vj = v_ref[pl.ds(start, TK), :]
        for hd in range(HEADS_PER_STEP):
            z = lax.dot_general(q_heads[hd], kj, _NT, preferred_element_type=F32)
            sp = _softplus(z)
            if mask is not None:
                sp = jnp.where(mask, sp, 0.0)
            hi, lo = _split_bf16(sp)
            suffix = _dot(hi, tri) + _dot(lo, tri)
            a = jnp.exp(z - suffix - c_scr[hd])
            if mask is not None:
                a = jnp.where(mask, a, 0.0)
            acc_scr[hd] = acc_scr[hd] + _dot(a.astype(BF16), vj)
            c_scr[hd] = c_scr[hd] + suffix[:, 0:1]

    acc_scr[...] = jnp.zeros_like(acc_scr)
    c_scr[...] = jnp.zeros_like(c_scr)
    row = lax.broadcasted_iota(jnp.int32, (tq, TK), 0)
    col = lax.broadcasted_iota(jnp.int32, (tq, TK), 1)
    visit(i, col < row)

    def body(it, carry):
        visit(i - 1 - it, None)
        return carry

    lax.fori_loop(0, i, body, 0)
    o_ref[...] = jnp.where(lane < HEAD_DIM, acc_scr[0], acc_scr[1]).astype(o_ref.dtype)


def _attn_call(q, k, v, tri, batch, seq_len):
    assert TQ == TK
    nq = seq_len // TQ
    pairs = N_HEADS // HEADS_PER_STEP
    qspec = pl.BlockSpec((TQ, LANES), lambda b, p, i: (b * nq + i, p))
    kvspec = pl.BlockSpec((seq_len, LANES), lambda b, p, i: (b, p))
    return pl.pallas_call(
        _attn_kernel,
        grid=(batch, pairs, nq),
        in_specs=[qspec, kvspec, kvspec, pl.BlockSpec((TK, TK), lambda b, p, i: (0, 0))],
        out_specs=qspec,
        out_shape=jax.ShapeDtypeStruct(q.shape, BF16),
        scratch_shapes=[pltpu.VMEM((HEADS_PER_STEP, TQ, LANES), F32),
                        pltpu.VMEM((HEADS_PER_STEP, TQ, 1), F32)],
        compiler_params=pltpu.CompilerParams(
            dimension_semantics=("arbitrary", "arbitrary", "arbitrary"),
            vmem_limit_bytes=VMEM_LIMIT),
        name="sb_attn",
    )(q, k, v, tri)


def _merge_kernel(attn_ref, y_ref, sa_ref, sb_ref, x_ref, wsb_ref, wcv_ref, wout_ref, g2_ref,
                  wrh_ref, wrl_ref, br_ref, x1_ref, h2_ref, route_ref):
    tm = x_ref.shape[0]
    branch_a = _dot(attn_ref[...], wsb_ref[...])
    branch_b = _dot(y_ref[...], wcv_ref[...])
    merged = (sa_ref[...] * branch_a + sb_ref[...] * branch_b).astype(BF16)
    x1 = x_ref[...] + _dot(merged, wout_ref[...])
    x1_ref[...] = x1
    ms = jnp.mean(x1 * x1, axis=-1, keepdims=True)
    h2 = x1 * lax.rsqrt(ms + EPS) * g2_ref[...]
    h2_ref[...] = h2.astype(BF16)

    hh, hl = _split_bf16(h2)
    nt = functools.partial(lax.dot_general, dimension_numbers=_NT, preferred_element_type=F32)
    logits = nt(wrh_ref[...], hh) + nt(wrh_ref[...], hl) + nt(wrl_ref[...], hh) + br_ref[...]

    rowi = lax.broadcasted_iota(jnp.int32, (SUBLANES, tm), 0).astype(F32)
    neg = jnp.float32(-jnp.inf)
    big = jnp.float32(SUBLANES)

    def first_argmax(vals):
        top = jnp.max(vals, axis=0, keepdims=True)
        idx = jnp.min(jnp.where(vals == top, rowi, big), axis=0, keepdims=True)
        return top, idx

    gl = jnp.where(rowi < N_GROUPS, logits[0:SUBLANES, :], neg)
    gmax, gidx = first_argmax(gl)
    g_w = 1.0 / jnp.sum(jnp.exp(gl - gmax), axis=0, keepdims=True)
    el = logits[SUBLANES:2 * SUBLANES, :]
    for g in range(1, N_GROUPS):
        el = jnp.where(gidx == g, logits[(g + 1) * SUBLANES:(g + 2) * SUBLANES, :], el)
    t1, i1 = first_argmax(el)
    t2, i2 = first_argmax(jnp.where(rowi == i1, neg, el))
    e = jnp.exp(t2 - t1)
    w1 = g_w / (1.0 + e)
    w2 = g_w * e / (1.0 + e)
    out = jnp.zeros((SUBLANES, tm), F32)
    for r, val in enumerate((gidx, i1, i2, w1, w2)):
        out = jnp.where(rowi == r, val, out)
    route_ref[...] = out


def _merge_call(attn, y, sa, sb, x2d, wsb, wcv, wout, g2, wrh, wrl, br):
    n = x2d.shape[0]
    tm = TM_PROJ
    rows = lambda width: pl.BlockSpec((tm, width), lambda i: (i, 0))
    const = lambda shape: pl.BlockSpec(shape, lambda i: (0, 0))
    return pl.pallas_call(
        _merge_kernel,
        grid=(n // tm,),
        in_specs=[rows(WIDTH), rows(WIDTH), rows(D_MODEL), rows(D_MODEL), rows(D_MODEL),
                  const((WIDTH, D_MODEL)), const((WIDTH, D_MODEL)), const((D_MODEL, D_MODEL)),
                  const((1, D_MODEL)), const((ROUTE_ROWS, D_MODEL)), const((ROUTE_ROWS, D_MODEL)),
                  const((ROUTE_ROWS, 1))],
        out_specs=[rows(D_MODEL), rows(D_MODEL), pl.BlockSpec((SUBLANES, tm), lambda i: (0, i))],
        out_shape=[jax.ShapeDtypeStruct((n, D_MODEL), F32), jax.ShapeDtypeStruct((n, D_MODEL), BF16),
                   jax.ShapeDtypeStruct((SUBLANES, n), F32)],
        compiler_params=pltpu.CompilerParams(dimension_semantics=("arbitrary",),
                                             vmem_limit_bytes=VMEM_LIMIT),
        name="merge_router",
    )(attn, y, sa, sb, x2d, wsb, wcv, wout, g2, wrh, wrl, br)


def _moe_kernel(tg_ref, tv_ref, xs_ref, cw_ref, wg_ref, wu_ref, wd_ref, y_ref):
    i = pl.program_id(0)

    @pl.when(tv_ref[i] > 0)
    def _():
        xs = xs_ref[...]
        cw = cw_ref[...]
        acc = jnp.zeros(y_ref.shape, F32)
        for e in range(E_PER_GROUP):
            gate = _dot(xs, wg_ref[e])
            up = _dot(xs, wu_ref[e])
            act = (gate * jax.nn.sigmoid(gate)) * up * cw[:, e:e + 1]
            acc = acc + _dot(act.astype(BF16), wd_ref[e])
        y_ref[...] = acc

    @pl.when(tv_ref[i] == 0)
    def _():
        y_ref[...] = jnp.zeros(y_ref.shape, F32)


def _moe_call(tile_group, tile_valid, xs, cw, wg, wu, wd):
    p = xs.shape[0]
    t = TM_MOE
    grid_spec = pltpu.PrefetchScalarGridSpec(
        num_scalar_prefetch=2,
        grid=(p // t,),
        in_specs=[pl.BlockSpec((t, D_MODEL), lambda i, tg, tv: (i, 0)),
                  pl.BlockSpec((t, E_PER_GROUP), lambda i, tg, tv: (i, 0)),
                  pl.BlockSpec((E_PER_GROUP, D_MODEL, D_EXPERT), lambda i, tg, tv: (tg[i], 0, 0)),
                  pl.BlockSpec((E_PER_GROUP, D_MODEL, D_EXPERT), lambda i, tg, tv: (tg[i], 0, 0)),
                  pl.BlockSpec((E_PER_GROUP, D_EXPERT, D_MODEL), lambda i, tg, tv: (tg[i], 0, 0))],
        out_specs=pl.BlockSpec((t, D_MODEL), lambda i, tg, tv: (i, 0)),
    )
    return pl.pallas_call(
        _moe_kernel,
        grid_spec=grid_spec,
        out_shape=jax.ShapeDtypeStruct((p, D_MODEL), F32),
        compiler_params=pltpu.CompilerParams(dimension_semantics=("arbitrary",),
                                             vmem_limit_bytes=VMEM_LIMIT),
        name="moe_experts",
    )(tile_group, tile_valid, xs, cw, wg, wu, wd)


def _router_rows(w_rg, b_rg, w_re, b_re):
    w = jnp.zeros((ROUTE_ROWS, D_MODEL), F32)
    w = w.at[0:N_GROUPS].set(w_rg.T).at[SUBLANES:SUBLANES + N_EXPERTS].set(w_re.T)
    b = jnp.zeros((ROUTE_ROWS, 1), F32)
    b = b.at[0:N_GROUPS, 0].set(b_rg).at[SUBLANES:SUBLANES + N_EXPERTS, 0].set(b_re)
    hi = w.astype(BF16)
    lo = (w - hi.astype(F32)).astype(BF16)
    return hi, lo, b


def _layer(x2d, batch, seq_len, norm_mix_g, w_in, q_norm_g, k_norm_g, conv_w, w_sb_branch,
           w_conv_branch, w_out, norm_ffn_g, w_router_group, b_router_group, w_router_expert,
           b_router_expert, w_gate_e, w_up_e, w_down_e):
    n = x2d.shape[0]
    lane = jnp.arange(WIDTH)
    head_mean = jnp.where(lane[:, None] // HEAD_DIM == lane[None, :] // HEAD_DIM,
                          1.0 / HEAD_DIM, 0.0).astype(BF16)
    qg = (jnp.tile(q_norm_g, N_HEADS) * HEAD_DIM ** -0.5)[None, :]
    kg = jnp.tile(k_norm_g, N_HEADS)[None, :]
    q, k, v, y, sa, sb = _inproj_call(x2d, norm_mix_g[None, :], w_in.astype(BF16), qg, kg,
                                      head_mean, conv_w, seq_len)

    pos = jnp.arange(TK)
    tri = (pos[:, None] >= pos[None, :]).astype(BF16)
    attn = _attn_call(q, k, v, tri, batch, seq_len)

    wrh, wrl, br = _router_rows(w_router_group, b_router_group, w_router_expert, b_router_expert)
    x1, h2, route = _merge_call(attn, y, sa, sb, x2d, w_sb_branch.astype(BF16),
                                w_conv_branch.astype(BF16), w_out.astype(BF16),
                                norm_ffn_g[None, :], wrh, wrl, br)

    t = TM_MOE
    grp = route[0].astype(jnp.int32)
    i1 = route[1].astype(jnp.int32)
    i2 = route[2].astype(jnp.int32)
    onehot = jax.nn.one_hot(grp, N_GROUPS, dtype=jnp.int32)
    csum = jnp.cumsum(onehot, axis=0)
    rank = jnp.sum(onehot * (csum - 1), axis=1)
    counts = csum[-1]
    padded = ((counts + t - 1) // t) * t
    ends = jnp.cumsum(padded)
    offs = ends - padded
    dest = offs[grp] + rank
    p_rows = n + N_GROUPS * t
    tok_of = jnp.zeros((p_rows,), jnp.int32).at[dest].set(jnp.arange(n, dtype=jnp.int32))
    row_ok = jnp.zeros((p_rows,), jnp.bool_).at[dest].set(True)
    cw_tok = (jax.nn.one_hot(i1, E_PER_GROUP, dtype=F32) * route[3][:, None]
              + jax.nn.one_hot(i2, E_PER_GROUP, dtype=F32) * route[4][:, None])
    xs = jnp.take(h2, tok_of, axis=0)
    cw = jnp.where(row_ok[:, None], jnp.take(cw_tok, tok_of, axis=0), 0.0)
    tile_start = jnp.arange(p_rows // t, dtype=jnp.int32) * t
    tile_group = jnp.minimum(jnp.searchsorted(ends, tile_start, side="right"),
                             N_GROUPS - 1).astype(jnp.int32)
    tile_valid = (tile_start < ends[-1]).astype(jnp.int32)
    last_group = tile_group[jnp.maximum(jnp.sum(tile_valid) - 1, 0)]
    tile_group = jnp.where(tile_valid > 0, tile_group, last_group)

    y_sorted = _moe_call(tile_group, tile_valid, xs, cw, w_gate_e.astype(BF16),
                         w_up_e.astype(BF16), w_down_e.astype(BF16))
    return x1 + jnp.take(y_sorted, dest, axis=0)


def kernel(x, norm_mix_g, w_in, q_norm_g, k_norm_g, conv_w, w_sb_branch, w_conv_branch, w_out,
           norm_ffn_g, w_router_group, b_router_group, w_router_expert, b_router_expert,
           w_gate_e, w_up_e, w_down_e):
    batch, seq_len, d = x.shape
    x2d = x.reshape(batch * seq_len, d)
    for l in range(norm_mix_g.shape[0]):
        x2d = _layer(x2d, batch, seq_len, norm_mix_g[l], w_in[l], q_norm_g[l], k_norm_g[l],
                     conv_w[l], w_sb_branch[l], w_conv_branch[l], w_out[l], norm_ffn_g[l],
                     w_router_group[l], b_router_group[l], w_router_expert[l], b_router_expert[l],
                     w_gate_e[l], w_up_e[l], w_down_e[l])
    return x2d.reshape(batch, seq_len, d)
```

```python
import functools

import jax
import jax.numpy as jnp
from jax import lax
from jax.experimental import pallas as pl
from jax.experimental.pallas import tpu as pltpu

F32 = jnp.float32
BF16 = jnp.bfloat16

D_MODEL = 1024
N_HEADS = 8
HEAD_DIM = 64
WIDTH = N_HEADS * HEAD_DIM
CONV_K = 3
N_GROUPS = 4
E_PER_GROUP = 8
N_EXPERTS = N_GROUPS * E_PER_GROUP
D_EXPERT = 256
EPS = 1e-6

LANES = 128
SUBLANES = 8
HEADS_PER_STEP = LANES // HEAD_DIM
N_PAIRS = N_HEADS // HEADS_PER_STEP

TM_PROJ = 256
TQ = 256
TK = 256
TM_MOE = 256
ROUTE_ROWS = 48
VMEM_LIMIT = 48 * 1024 * 1024

_NT = (((1,), (1,)), ((), ()))


def _dot(a, b):
    return jnp.dot(a, b, preferred_element_type=F32)


def _split_bf16(v):
    hi = v.astype(BF16)
    lo = (v - hi.astype(F32)).astype(BF16)
    return hi, lo


def _inproj_kernel(x_ref, g1_ref, w_ref, qg_ref, kg_ref, pm_ref, cw_ref,
                   q_ref, k_ref, v_ref, y_ref, sa_ref, sb_ref, cu_scr, *, tiles_per_seq):
    tm = x_ref.shape[0]
    x = x_ref[...]
    ms = jnp.mean(x * x, axis=-1, keepdims=True)
    h = (x * lax.rsqrt(ms + EPS) * g1_ref[...]).astype(BF16)

    def proj(lo, width):
        return _dot(h, w_ref[:, lo:lo + width])

    def head_norm(t, g_ref):
        hms = _dot((t * t).astype(BF16), pm_ref[...])
        return (t * lax.rsqrt(hms + EPS) * g_ref[...]).astype(BF16)

    q_ref[...] = head_norm(proj(0, WIDTH), qg_ref)
    k_ref[...] = head_norm(proj(WIDTH, WIDTH), kg_ref)
    v_ref[...] = proj(2 * WIDTH, WIDTH).astype(BF16)

    c_b = proj(3 * WIDTH, WIDTH)
    cu = proj(4 * WIDTH, WIDTH) * proj(5 * WIDTH, WIDTH)

    @pl.when(pl.program_id(0) % tiles_per_seq == 0)
    def _():
        cu_scr[0:SUBLANES, :] = jnp.zeros((SUBLANES, WIDTH), F32)

    cu_scr[SUBLANES:SUBLANES + tm, :] = cu
    cw = cw_ref[...]
    conv = (cw[0:1, :] * cu_scr[SUBLANES - 2:SUBLANES - 2 + tm, :]
            + cw[1:2, :] * cu_scr[SUBLANES - 1:SUBLANES - 1 + tm, :]
            + cw[2:3, :] * cu)
    y_ref[...] = (c_b * conv).astype(BF16)
    cu_scr[0:SUBLANES, :] = cu_scr[tm:tm + SUBLANES, :]

    sa_ref[...] = jax.nn.sigmoid(proj(6 * WIDTH, D_MODEL))
    sb_ref[...] = jax.nn.sigmoid(proj(6 * WIDTH + D_MODEL, D_MODEL))


def _inproj_call(x2d, g1, w_in, qg, kg, pm, conv_w, seq_len):
    n = x2d.shape[0]
    tm = TM_PROJ
    in_w = w_in.shape[1]
    const = lambda shape: pl.BlockSpec(shape, lambda i: (0, 0))
    rows = lambda width: pl.BlockSpec((tm, width), lambda i: (i, 0))
    return pl.pallas_call(
        functools.partial(_inproj_kernel, tiles_per_seq=seq_len // tm),
        grid=(n // tm,),
        in_specs=[rows(D_MODEL), const((1, D_MODEL)), const((D_MODEL, in_w)), const((1, WIDTH)),
                  const((1, WIDTH)), const((WIDTH, WIDTH)), const((CONV_K, WIDTH))],
        out_specs=[rows(WIDTH), rows(WIDTH), rows(WIDTH), rows(WIDTH), rows(D_MODEL), rows(D_MODEL)],
        out_shape=[jax.ShapeDtypeStruct((n, WIDTH), BF16)] * 4
                  + [jax.ShapeDtypeStruct((n, D_MODEL), F32)] * 2,
        scratch_shapes=[pltpu.VMEM((SUBLANES + tm, WIDTH), F32)],
        compiler_params=pltpu.CompilerParams(dimension_semantics=("arbitrary",),
                                             vmem_limit_bytes=VMEM_LIMIT),
        name="inproj",
    )(x2d, g1, w_in, qg, kg, pm, conv_w)


def _softplus(z):
    return jnp.maximum(z, 0.0) + jnp.log(1.0 + jnp.exp(-jnp.abs(z)))


def _attn_kernel(q_ref, k_ref, v_ref, tri_ref, o_ref, qm_scr, z_scr, suf_scr, acc_scr, c_scr):
    i = pl.program_id(1)
    tq = q_ref.shape[0]
    lane = lax.broadcasted_iota(jnp.int32, (tq, LANES), 1)
    left = lane < HEAD_DIM
    for p in range(N_PAIRS):
        q2 = q_ref[:, p * LANES:(p + 1) * LANES]
        zero = jnp.zeros_like(q2)
        qm_scr[2 * p] = jnp.where(left, q2, zero)
        qm_scr[2 * p + 1] = jnp.where(left, zero, q2)
    acc_scr[...] = jnp.zeros_like(acc_scr)
    c_scr[...] = jnp.zeros_like(c_scr)

    def visit(j, mask):
        start = pl.multiple_of(j * TK, TK)
        for s in range(N_HEADS):
            p = s // HEADS_PER_STEP
            kj = k_ref[pl.ds(start, TK), p * LANES:(p + 1) * LANES]
            z_scr[s] = lax.dot_general(qm_scr[s], kj, _NT, preferred_element_type=F32)
        for s in range(N_HEADS):
            sp = _softplus(z_scr[s])
            if mask is not None:
                sp = jnp.where(mask, sp, 0.0)
            hi, lo = _split_bf16(sp)
            suf_scr[s] = _dot(hi, tri_ref[...]) + _dot(lo, tri_ref[...])
        for s in range(N_HEADS):
            p = s // HEADS_PER_STEP
            vj = v_ref[pl.ds(start, TK), p * LANES:(p + 1) * LANES]
            suffix = suf_scr[s]
            a = jnp.exp(z_scr[s] - suffix - c_scr[s])
            if mask is not None:
                a = jnp.where(mask, a, 0.0)
            acc_scr[s] = acc_scr[s] + _dot(a.astype(BF16), vj)
            c_scr[s] = c_scr[s] + suffix[:, 0:1]

    row = lax.broadcasted_iota(jnp.int32, (tq, TK), 0)
    col = lax.broadcasted_iota(jnp.int32, (tq, TK), 1)
    visit(i, col < row)

    def body(it, carry):
        visit(i - 1 - it, None)
        return carry

    lax.fori_loop(0, i, body, 0)
    for p in range(N_PAIRS):
        o_ref[:, p * LANES:(p + 1) * LANES] = jnp.where(
            left, acc_scr[2 * p], acc_scr[2 * p + 1]).astype(o_ref.dtype)


def _attn_call(q, k, v, tri, batch, seq_len):
    assert TQ == TK
    nq = seq_len // TQ
    qspec = pl.BlockSpec((TQ, WIDTH), lambda b, i: (b * nq + i, 0))
    kvspec = pl.BlockSpec((seq_len, WIDTH), lambda b, i: (b, 0))
    return pl.pallas_call(
        _attn_kernel,
        grid=(batch, nq),
        in_specs=[qspec, kvspec, kvspec, pl.BlockSpec((TK, TK), lambda b, i: (0, 0))],
        out_specs=qspec,
        out_shape=jax.ShapeDtypeStruct(q.shape, BF16),
        scratch_shapes=[pltpu.VMEM((N_HEADS, TQ, LANES), BF16),
                        pltpu.VMEM((N_HEADS, TQ, TK), F32),
                        pltpu.VMEM((N_HEADS, TQ, TK), F32),
                        pltpu.VMEM((N_HEADS, TQ, LANES), F32),
                        pltpu.VMEM((N_HEADS, TQ, 1), F32)],
        compiler_params=pltpu.CompilerParams(dimension_semantics=("arbitrary", "arbitrary"),
                                             vmem_limit_bytes=VMEM_LIMIT),
        name="sb_attn",
    )(q, k, v, tri)


def _merge_kernel(attn_ref, y_ref, sa_ref, sb_ref, x_ref, wsb_ref, wcv_ref, wout_ref, g2_ref,
                  wrh_ref, wrl_ref, br_ref, x1_ref, h2_ref, route_ref):
    tm = x_ref.shape[0]
    branch_a = _dot(attn_ref[...], wsb_ref[...])
    branch_b = _dot(y_ref[...], wcv_ref[...])
    merged = (sa_ref[...] * branch_a + sb_ref[...] * branch_b).astype(BF16)
    x1 = x_ref[...] + _dot(merged, wout_ref[...])
    x1_ref[...] = x1
    ms = jnp.mean(x1 * x1, axis=-1, keepdims=True)
    h2 = x1 * lax.rsqrt(ms + EPS) * g2_ref[...]
    h2_ref[...] = h2.astype(BF16)

    hh, hl = _split_bf16(h2)
    nt = functools.partial(lax.dot_general, dimension_numbers=_NT, preferred_element_type=F32)
    logits = nt(wrh_ref[...], hh) + nt(wrh_ref[...], hl) + nt(wrl_ref[...], hh) + br_ref[...]

    rowi = lax.broadcasted_iota(jnp.int32, (SUBLANES, tm), 0).astype(F32)
    neg = jnp.float32(-jnp.inf)
    big = jnp.float32(SUBLANES)

    def first_argmax(vals):
        top = jnp.max(vals, axis=0, keepdims=True)
        idx = jnp.min(jnp.where(vals == top, rowi, big), axis=0, keepdims=True)
        return top, idx

    gl = jnp.where(rowi < N_GROUPS, logits[0:SUBLANES, :], neg)
    gmax, gidx = first_argmax(gl)
    g_w = 1.0 / jnp.sum(jnp.exp(gl - gmax), axis=0, keepdims=True)
    el = logits[SUBLANES:2 * SUBLANES, :]
    for g in range(1, N_GROUPS):
        el = jnp.where(gidx == g, logits[(g + 1) * SUBLANES:(g + 2) * SUBLANES, :], el)
    t1, i1 = first_argmax(el)
    t2, i2 = first_argmax(jnp.where(rowi == i1, neg, el))
    e = jnp.exp(t2 - t1)
    w1 = g_w / (1.0 + e)
    w2 = g_w * e / (1.0 + e)
    out = jnp.zeros((SUBLANES, tm), F32)
    for r, val in enumerate((gidx, i1, i2, w1, w2)):
        out = jnp.where(rowi == r, val, out)
    route_ref[...] = out


def _merge_call(attn, y, sa, sb, x2d, wsb, wcv, wout, g2, wrh, wrl, br):
    n = x2d.shape[0]
    tm = TM_PROJ
    rows = lambda width: pl.BlockSpec((tm, width), lambda i: (i, 0))
    const = lambda shape: pl.BlockSpec(shape, lambda i: (0, 0))
    return pl.pallas_call(
        _merge_kernel,
        grid=(n // tm,),
        in_specs=[rows(WIDTH), rows(WIDTH), rows(D_MODEL), rows(D_MODEL), rows(D_MODEL),
                  const((WIDTH, D_MODEL)), const((WIDTH, D_MODEL)), const((D_MODEL, D_MODEL)),
                  const((1, D_MODEL)), const((ROUTE_ROWS, D_MODEL)), const((ROUTE_ROWS, D_MODEL)),
                  const((ROUTE_ROWS, 1))],
        out_specs=[rows(D_MODEL), rows(D_MODEL), pl.BlockSpec((SUBLANES, tm), lambda i: (0, i))],
        out_shape=[jax.ShapeDtypeStruct((n, D_MODEL), F32), jax.ShapeDtypeStruct((n, D_MODEL), BF16),
                   jax.ShapeDtypeStruct((SUBLANES, n), F32)],
        compiler_params=pltpu.CompilerParams(dimension_semantics=("arbitrary",),
                                             vmem_limit_bytes=VMEM_LIMIT),
        name="merge_router",
    )(attn, y, sa, sb, x2d, wsb, wcv, wout, g2, wrh, wrl, br)


def _moe_kernel(tg_ref, tv_ref, xs_ref, cw_ref, wg_ref, wu_ref, wd_ref, y_ref):
    i = pl.program_id(0)

    @pl.when(tv_ref[i] > 0)
    def _():
        xs = xs_ref[...]
        cw = cw_ref[...]
        acc = jnp.zeros(y_ref.shape, F32)
        for e in range(E_PER_GROUP):
            gate = _dot(xs, wg_ref[e])
            up = _dot(xs, wu_ref[e])
            act = (gate * jax.nn.sigmoid(gate)) * up * cw[:, e:e + 1]
            acc = acc + _dot(act.astype(BF16), wd_ref[e])
        y_ref[...] = acc

    @pl.when(tv_ref[i] == 0)
    def _():
        y_ref[...] = jnp.zeros(y_ref.shape, F32)


def _moe_call(tile_group, tile_valid, xs, cw, wg, wu, wd):
    p = xs.shape[0]
    t = TM_MOE
    grid_spec = pltpu.PrefetchScalarGridSpec(
        num_scalar_prefetch=2,
        grid=(p // t,),
        in_specs=[pl.BlockSpec((t, D_MODEL), lambda i, tg, tv: (i, 0)),
                  pl.BlockSpec((t, E_PER_GROUP), lambda i, tg, tv: (i, 0)),
                  pl.BlockSpec((E_PER_GROUP, D_MODEL, D_EXPERT), lambda i, tg, tv: (tg[i], 0, 0)),
                  pl.BlockSpec((E_PER_GROUP, D_MODEL, D_EXPERT), lambda i, tg, tv: (tg[i], 0, 0)),
                  pl.BlockSpec((E_PER_GROUP, D_EXPERT, D_MODEL), lambda i, tg, tv: (tg[i], 0, 0))],
        out_specs=pl.BlockSpec((t, D_MODEL), lambda i, tg, tv: (i, 0)),
    )
    return pl.pallas_call(
        _moe_kernel,
        grid_spec=grid_spec,
        out_shape=jax.ShapeDtypeStruct((p, D_MODEL), F32),
        compiler_params=pltpu.CompilerParams(dimension_semantics=("arbitrary",),
                                             vmem_limit_bytes=VMEM_LIMIT),
        name="moe_experts",
    )(tile_group, tile_valid, xs, cw, wg, wu, wd)


def _router_rows(w_rg, b_rg, w_re, b_re):
    w = jnp.zeros((ROUTE_ROWS, D_MODEL), F32)
    w = w.at[0:N_GROUPS].set(w_rg.T).at[SUBLANES:SUBLANES + N_EXPERTS].set(w_re.T)
    b = jnp.zeros((ROUTE_ROWS, 1), F32)
    b = b.at[0:N_GROUPS, 0].set(b_rg).at[SUBLANES:SUBLANES + N_EXPERTS, 0].set(b_re)
    hi = w.astype(BF16)
    lo = (w - hi.astype(F32)).astype(BF16)
    return hi, lo, b


def _layer(x2d, batch, seq_len, norm_mix_g, w_in, q_norm_g, k_norm_g, conv_w, w_sb_branch,
           w_conv_branch, w_out, norm_ffn_g, w_router_group, b_router_group, w_router_expert,
           b_router_expert, w_gate_e, w_up_e, w_down_e):
    n = x2d.shape[0]
    lane = jnp.arange(WIDTH)
    head_mean = jnp.where(lane[:, None] // HEAD_DIM == lane[None, :] // HEAD_DIM,
                          1.0 / HEAD_DIM, 0.0).astype(BF16)
    qg = (jnp.tile(q_norm_g, N_HEADS) * HEAD_DIM ** -0.5)[None, :]
    kg = jnp.tile(k_norm_g, N_HEADS)[None, :]
    q, k, v, y, sa, sb = _inproj_call(x2d, norm_mix_g[None, :], w_in.astype(BF16), qg, kg,
                                      head_mean, conv_w, seq_len)

    pos = jnp.arange(TK)
    tri = (pos[:, None] >= pos[None, :]).astype(BF16)
    attn = _attn_call(q, k, v, tri, batch, seq_len)

    wrh, wrl, br = _router_rows(w_router_group, b_router_group, w_router_expert, b_router_expert)
    x1, h2, route = _merge_call(attn, y, sa, sb, x2d, w_sb_branch.astype(BF16),
                                w_conv_branch.astype(BF16), w_out.astype(BF16),
                                norm_ffn_g[None, :], wrh, wrl, br)

    t = TM_MOE
    grp = route[0].astype(jnp.int32)
    i1 = route[1].astype(jnp.int32)
    i2 = route[2].astype(jnp.int32)
    onehot = jax.nn.one_hot(grp, N_GROUPS, dtype=jnp.int32)
    csum = jnp.cumsum(onehot, axis=0)
    rank = jnp.sum(onehot * (csum - 1), axis=1)
    counts = csum[-1]
    padded = ((counts + t - 1) // t) * t
    ends = jnp.cumsum(padded)
    offs = ends - padded
    dest = offs[grp] + rank
    p_rows = n + N_GROUPS * t
    tok_of = jnp.zeros((p_rows,), jnp.int32).at[dest].set(jnp.arange(n, dtype=jnp.int32))
    row_ok = jnp.zeros((p_rows,), jnp.bool_).at[dest].set(True)
    cw_tok = (jax.nn.one_hot(i1, E_PER_GROUP, dtype=F32) * route[3][:, None]
              + jax.nn.one_hot(i2, E_PER_GROUP, dtype=F32) * route[4][:, None])
    xs = jnp.take(h2, tok_of, axis=0)
    cw = jnp.where(row_ok[:, None], jnp.take(cw_tok, tok_of, axis=0), 0.0)
    tile_start = jnp.arange(p_rows // t, dtype=jnp.int32) * t
    tile_group = jnp.minimum(jnp.searchsorted(ends, tile_start, side="right"),
                             N_GROUPS - 1).astype(jnp.int32)
    tile_valid = (tile_start < ends[-1]).astype(jnp.int32)
    last_group = tile_group[jnp.maximum(jnp.sum(tile_valid) - 1, 0)]
    tile_group = jnp.where(tile_valid > 0, tile_group, last_group)

    y_sorted = _moe_call(tile_group, tile_valid, xs, cw, w_gate_e.astype(BF16),
                         w_up_e.astype(BF16), w_down_e.astype(BF16))
    return x1 + jnp.take(y_sorted, dest, axis=0)


def kernel(x, norm_mix_g, w_in, q_norm_g, k_norm_g, conv_w, w_sb_branch, w_conv_branch, w_out,
           norm_ffn_g, w_router_group, b_router_group, w_router_expert, b_router_expert,
           w_gate_e, w_up_e, w_down_e):
    batch, seq_len, d = x.shape
    x2d = x.reshape(batch * seq_len, d)
    for l in range(norm_mix_g.shape[0]):
        x2d = _layer(x2d, batch, seq_len, norm_mix_g[l], w_in[l], q_norm_g[l], k_norm_g[l],
                     conv_w[l], w_sb_branch[l], w_conv_branch[l], w_out[l], norm_ffn_g[l],
                     w_router_group[l], b_router_group[l], w_router_expert[l], b_router_expert[l],
                     w_gate_e[l], w_up_e[l], w_down_e[l])
    return x2d.reshape(batch, seq_len, d)
```

```python
import functools

import jax
import jax.numpy as jnp
from jax import lax
from jax.experimental import pallas as pl
from jax.experimental.pallas import tpu as pltpu

F32 = jnp.float32
BF16 = jnp.bfloat16

D_MODEL = 1024
N_HEADS = 8
HEAD_DIM = 64
WIDTH = N_HEADS * HEAD_DIM
CONV_K = 3
N_GROUPS = 4
E_PER_GROUP = 8
N_EXPERTS = N_GROUPS * E_PER_GROUP
D_EXPERT = 256
EPS = 1e-6

LANES = 128
SUBLANES = 8
HEADS_PER_STEP = LANES // HEAD_DIM
N_PAIRS = N_HEADS // HEADS_PER_STEP

TM_PROJ = 256
TQ = 256
TK = 256
TM_MOE = 256
TM_PERM = 512
ROW_W = D_MODEL + LANES
ISSUE_UNROLL = 8
ROUTE_ROWS = 48
VMEM_LIMIT = 48 * 1024 * 1024

_NT = (((1,), (1,)), ((), ()))


def _dot(a, b):
    return jnp.dot(a, b, preferred_element_type=F32)


def _split_bf16(v):
    hi = v.astype(BF16)
    lo = (v - hi.astype(F32)).astype(BF16)
    return hi, lo


def _inproj_kernel(x_ref, g1_ref, w_ref, qg_ref, kg_ref, pm_ref, cw_ref,
                   q_ref, k_ref, v_ref, y_ref, sa_ref, sb_ref, cu_scr, *, tiles_per_seq):
    tm = x_ref.shape[0]
    x = x_ref[...]
    ms = jnp.mean(x * x, axis=-1, keepdims=True)
    h = (x * lax.rsqrt(ms + EPS) * g1_ref[...]).astype(BF16)

    def proj(lo, width):
        return _dot(h, w_ref[:, lo:lo + width])

    def head_norm(t, g_ref):
        hms = _dot((t * t).astype(BF16), pm_ref[...])
        return (t * lax.rsqrt(hms + EPS) * g_ref[...]).astype(BF16)

    q_ref[...] = head_norm(proj(0, WIDTH), qg_ref)
    k_ref[...] = head_norm(proj(WIDTH, WIDTH), kg_ref)
    v_ref[...] = proj(2 * WIDTH, WIDTH).astype(BF16)

    c_b = proj(3 * WIDTH, WIDTH)
    cu = proj(4 * WIDTH, WIDTH) * proj(5 * WIDTH, WIDTH)

    @pl.when(pl.program_id(0) % tiles_per_seq == 0)
    def _():
        cu_scr[0:SUBLANES, :] = jnp.zeros((SUBLANES, WIDTH), F32)

    cu_scr[SUBLANES:SUBLANES + tm, :] = cu
    cw = cw_ref[...]
    conv = (cw[0:1, :] * cu_scr[SUBLANES - 2:SUBLANES - 2 + tm, :]
            + cw[1:2, :] * cu_scr[SUBLANES - 1:SUBLANES - 1 + tm, :]
            + cw[2:3, :] * cu)
    y_ref[...] = (c_b * conv).astype(BF16)
    cu_scr[0:SUBLANES, :] = cu_scr[tm:tm + SUBLANES, :]

    sa_ref[...] = jax.nn.sigmoid(proj(6 * WIDTH, D_MODEL))
    sb_ref[...] = jax.nn.sigmoid(proj(6 * WIDTH + D_MODEL, D_MODEL))


def _inproj_call(x2d, g1, w_in, qg, kg, pm, conv_w, seq_len):
    n = x2d.shape[0]
    tm = TM_PROJ
    in_w = w_in.shape[1]
    const = lambda shape: pl.BlockSpec(shape, lambda i: (0, 0))
    rows = lambda width: pl.BlockSpec((tm, width), lambda i: (i, 0))
    return pl.pallas_call(
        functools.partial(_inproj_kernel, tiles_per_seq=seq_len // tm),
        grid=(n // tm,),
        in_specs=[rows(D_MODEL), const((1, D_MODEL)), const((D_MODEL, in_w)), const((1, WIDTH)),
                  const((1, WIDTH)), const((WIDTH, WIDTH)), const((CONV_K, WIDTH))],
        out_specs=[rows(WIDTH), rows(WIDTH), rows(WIDTH), rows(WIDTH), rows(D_MODEL), rows(D_MODEL)],
        out_shape=[jax.ShapeDtypeStruct((n, WIDTH), BF16)] * 4
                  + [jax.ShapeDtypeStruct((n, D_MODEL), F32)] * 2,
        scratch_shapes=[pltpu.VMEM((SUBLANES + tm, WIDTH), F32)],
        compiler_params=pltpu.CompilerParams(dimension_semantics=("arbitrary",),
                                             vmem_limit_bytes=VMEM_LIMIT),
        name="inproj",
    )(x2d, g1, w_in, qg, kg, pm, conv_w)


def _softplus(z):
    return jnp.maximum(z, 0.0) + jnp.log(1.0 + jnp.exp(-jnp.abs(z)))


def _attn_kernel(q_ref, k_ref, v_ref, tri_ref, o_ref, qm_scr, z_scr, suf_scr, acc_scr, c_scr):
    i = pl.program_id(1)
    tq = q_ref.shape[0]
    lane = lax.broadcasted_iota(jnp.int32, (tq, LANES), 1)
    left = lane < HEAD_DIM
    for p in range(N_PAIRS):
        q2 = q_ref[:, p * LANES:(p + 1) * LANES]
        zero = jnp.zeros_like(q2)
        qm_scr[2 * p] = jnp.where(left, q2, zero)
        qm_scr[2 * p + 1] = jnp.where(left, zero, q2)
    acc_scr[...] = jnp.zeros_like(acc_scr)
    c_scr[...] = jnp.zeros_like(c_scr)

    def visit(j, mask):
        start = pl.multiple_of(j * TK, TK)
        for s in range(N_HEADS):
            p = s // HEADS_PER_STEP
            kj = k_ref[pl.ds(start, TK), p * LANES:(p + 1) * LANES]
            z_scr[s] = lax.dot_general(qm_scr[s], kj, _NT, preferred_element_type=F32)
        for s in range(N_HEADS):
            sp = _softplus(z_scr[s])
            if mask is not None:
                sp = jnp.where(mask, sp, 0.0)
            suf_scr[s] = _dot(sp.astype(BF16), tri_ref[...])
        for s in range(N_HEADS):
            p = s // HEADS_PER_STEP
            vj = v_ref[pl.ds(start, TK), p * LANES:(p + 1) * LANES]
            suffix = suf_scr[s]
            a = jnp.exp(z_scr[s] - suffix - c_scr[s])
            if mask is not None:
                a = jnp.where(mask, a, 0.0)
            acc_scr[s] = acc_scr[s] + _dot(a.astype(BF16), vj)
            c_scr[s] = c_scr[s] + suffix[:, 0:1]

    row = lax.broadcasted_iota(jnp.int32, (tq, TK), 0)
    col = lax.broadcasted_iota(jnp.int32, (tq, TK), 1)
    visit(i, col < row)

    def body(it, carry):
        visit(i - 1 - it, None)
        return carry

    lax.fori_loop(0, i, body, 0)
    for p in range(N_PAIRS):
        o_ref[:, p * LANES:(p + 1) * LANES] = jnp.where(
            left, acc_scr[2 * p], acc_scr[2 * p + 1]).astype(o_ref.dtype)


def _attn_call(q, k, v, tri, batch, seq_len):
    assert TQ == TK
    nq = seq_len // TQ
    qspec = pl.BlockSpec((TQ, WIDTH), lambda b, i: (b * nq + i, 0))
    kvspec = pl.BlockSpec((seq_len, WIDTH), lambda b, i: (b, 0))
    return pl.pallas_call(
        _attn_kernel,
        grid=(batch, nq),
        in_specs=[qspec, kvspec, kvspec, pl.BlockSpec((TK, TK), lambda b, i: (0, 0))],
        out_specs=qspec,
        out_shape=jax.ShapeDtypeStruct(q.shape, BF16),
        scratch_shapes=[pltpu.VMEM((N_HEADS, TQ, LANES), BF16),
                        pltpu.VMEM((N_HEADS, TQ, TK), F32),
                        pltpu.VMEM((N_HEADS, TQ, TK), F32),
                        pltpu.VMEM((N_HEADS, TQ, LANES), F32),
                        pltpu.VMEM((N_HEADS, TQ, 1), F32)],
        compiler_params=pltpu.CompilerParams(dimension_semantics=("arbitrary", "arbitrary"),
                                             vmem_limit_bytes=VMEM_LIMIT),
        name="sb_attn",
    )(q, k, v, tri)


def _merge_kernel(attn_ref, y_ref, sa_ref, sb_ref, x_ref, wsb_ref, wcv_ref, wout_ref, g2_ref,
                  wrh_ref, wrl_ref, br_ref, upper_ref, x1e_ref, route_ref, cnt_ref, cnt_scr):
    tm = x_ref.shape[0]
    branch_a = _dot(attn_ref[...], wsb_ref[...])
    branch_b = _dot(y_ref[...], wcv_ref[...])
    merged = (sa_ref[...] * branch_a + sb_ref[...] * branch_b).astype(BF16)
    x1 = x_ref[...] + _dot(merged, wout_ref[...])
    x1e_ref[:, 0:D_MODEL] = x1
    ms = jnp.mean(x1 * x1, axis=-1, keepdims=True)
    h2 = x1 * lax.rsqrt(ms + EPS) * g2_ref[...]

    hh, hl = _split_bf16(h2)
    nt = functools.partial(lax.dot_general, dimension_numbers=_NT, preferred_element_type=F32)
    logits = nt(wrh_ref[...], hh) + nt(wrh_ref[...], hl) + nt(wrl_ref[...], hh) + br_ref[...]

    rowi = lax.broadcasted_iota(jnp.int32, (SUBLANES, tm), 0).astype(F32)
    neg = jnp.float32(-jnp.inf)
    big = jnp.float32(SUBLANES)

    def first_argmax(vals):
        top = jnp.max(vals, axis=0, keepdims=True)
        idx = jnp.min(jnp.where(vals == top, rowi, big), axis=0, keepdims=True)
        return top, idx

    gl = jnp.where(rowi < N_GROUPS, logits[0:SUBLANES, :], neg)
    gmax, gidx = first_argmax(gl)
    g_w = 1.0 / jnp.sum(jnp.exp(gl - gmax), axis=0, keepdims=True)
    el = logits[SUBLANES:2 * SUBLANES, :]
    for g in range(1, N_GROUPS):
        el = jnp.where(gidx == g, logits[(g + 1) * SUBLANES:(g + 2) * SUBLANES, :], el)
    t1, i1 = first_argmax(el)
    t2, i2 = first_argmax(jnp.where(rowi == i1, neg, el))
    e = jnp.exp(t2 - t1)
    w1 = g_w / (1.0 + e)
    w2 = g_w * e / (1.0 + e)

    cw_t = jnp.where(rowi == i1, w1, 0.0) + jnp.where(rowi == i2, w2, 0.0)
    cw_t = jnp.concatenate([cw_t, jnp.zeros((LANES - SUBLANES, tm), F32)], axis=0)
    x1e_ref[:, D_MODEL:D_MODEL + LANES] = cw_t.T

    @pl.when(pl.program_id(0) == 0)
    def _():
        cnt_scr[...] = jnp.zeros_like(cnt_scr)

    onehot = jnp.where(rowi == gidx, 1.0, 0.0)
    before = _dot(onehot.astype(BF16), upper_ref[...]) + cnt_scr[:, 0:1]
    rank = jnp.sum(onehot * before, axis=0, keepdims=True)
    cnt_scr[...] = cnt_scr[...] + jnp.sum(onehot, axis=1, keepdims=True)
    cnt_ref[...] = cnt_scr[...]
    route_ref[...] = jnp.where(rowi == 0, gidx, jnp.where(rowi == 1, rank, 0.0))


def _merge_call(attn, y, sa, sb, x2d, wsb, wcv, wout, g2, wrh, wrl, br, upper):
    n = x2d.shape[0]
    tm = TM_PROJ
    rows = lambda width: pl.BlockSpec((tm, width), lambda i: (i, 0))
    const = lambda shape: pl.BlockSpec(shape, lambda i: (0, 0))
    return pl.pallas_call(
        _merge_kernel,
        grid=(n // tm,),
        in_specs=[rows(WIDTH), rows(WIDTH), rows(D_MODEL), rows(D_MODEL), rows(D_MODEL),
                  const((WIDTH, D_MODEL)), const((WIDTH, D_MODEL)), const((D_MODEL, D_MODEL)),
                  const((1, D_MODEL)), const((ROUTE_ROWS, D_MODEL)), const((ROUTE_ROWS, D_MODEL)),
                  const((ROUTE_ROWS, 1)), const((tm, tm))],
        out_specs=[rows(ROW_W), pl.BlockSpec((SUBLANES, tm), lambda i: (0, i)),
                   const((SUBLANES, LANES))],
        out_shape=[jax.ShapeDtypeStruct((n, ROW_W), F32), jax.ShapeDtypeStruct((SUBLANES, n), F32),
                   jax.ShapeDtypeStruct((SUBLANES, LANES), F32)],
        scratch_shapes=[pltpu.VMEM((SUBLANES, LANES), F32)],
        compiler_params=pltpu.CompilerParams(dimension_semantics=("arbitrary",),
                                             vmem_limit_bytes=VMEM_LIMIT),
        name="merge_router",
    )(attn, y, sa, sb, x2d, wsb, wcv, wout, g2, wrh, wrl, br, upper)


def _row_copy(src_ref, src_row, dst_ref, dst_row, sem):
    return pltpu.make_async_copy(src_ref.at[pl.ds(src_row, 1)], dst_ref.at[pl.ds(dst_row, 1)], sem)


def _scatter_kernel(dest_ref, fill_start_ref, fill_n_ref, x_ref, xs_ref, zero_scr, sem):
    i = pl.program_id(0)
    tm = x_ref.shape[0]
    base = i * tm

    def start(r, carry):
        _row_copy(x_ref, r, xs_ref, dest_ref[base + r], sem).start()
        return carry

    lax.fori_loop(0, tm, start, 0, unroll=ISSUE_UNROLL)
    pltpu.make_async_copy(x_ref, xs_ref.at[pl.ds(0, tm)], sem).wait()

    @pl.when(i == pl.num_programs(0) - 1)
    def _():
        zero_scr[...] = jnp.zeros_like(zero_scr)
        for g in range(N_GROUPS + 1):
            first = fill_start_ref[g]

            def start_fill(r, carry):
                _row_copy(zero_scr, 0, xs_ref, first + r, sem).start()
                return carry

            def wait_fill(r, carry):
                _row_copy(zero_scr, 0, xs_ref, 0, sem).wait()
                return carry

            lax.fori_loop(0, fill_n_ref[g], start_fill, 0)
            lax.fori_loop(0, fill_n_ref[g], wait_fill, 0)


def _scatter_call(dest, fill_start, fill_n, x1e, p_rows):
    n = x1e.shape[0]
    tm = TM_PERM
    grid_spec = pltpu.PrefetchScalarGridSpec(
        num_scalar_prefetch=3,
        grid=(n // tm,),
        in_specs=[pl.BlockSpec((tm, ROW_W), lambda i, *_: (i, 0))],
        out_specs=pl.BlockSpec(memory_space=pl.ANY),
        scratch_shapes=[pltpu.VMEM((SUBLANES, ROW_W), F32), pltpu.SemaphoreType.DMA(())],
    )
    return pl.pallas_call(
        _scatter_kernel,
        grid_spec=grid_spec,
        out_shape=jax.ShapeDtypeStruct((p_rows, ROW_W), F32),
        compiler_params=pltpu.CompilerParams(dimension_semantics=("arbitrary",),
                                             vmem_limit_bytes=VMEM_LIMIT),
        name="row_scatter",
    )(dest, fill_start, fill_n, x1e)


def _moe_kernel(tg_ref, tv_ref, xs_ref, g2_ref, wg_ref, wu_ref, wd_ref, o_ref):
    i = pl.program_id(0)

    @pl.when(tv_ref[i] == 0)
    def _():
        o_ref[...] = jnp.zeros_like(o_ref)

    @pl.when(tv_ref[i] > 0)
    def _():
        x1 = xs_ref[:, 0:D_MODEL]
        cw = xs_ref[:, D_MODEL:D_MODEL + LANES]
        ms = jnp.mean(x1 * x1, axis=-1, keepdims=True)
        h2 = (x1 * lax.rsqrt(ms + EPS) * g2_ref[...]).astype(BF16)
        acc = x1
        for e in range(E_PER_GROUP):
            gate = _dot(h2, wg_ref[e])
            up = _dot(h2, wu_ref[e])
            act = (gate * jax.nn.sigmoid(gate)) * up * cw[:, e:e + 1]
            acc = acc + _dot(act.astype(BF16), wd_ref[e])
        o_ref[...] = acc


def _moe_call(tile_group, tile_valid, xs, g2, wg, wu, wd):
    p = xs.shape[0]
    t = TM_MOE
    by_block = lambda i, tg, tv: (i, 0)
    by_group = lambda i, tg, tv: (tg[i], 0, 0)
    grid_spec = pltpu.PrefetchScalarGridSpec(
        num_scalar_prefetch=2,
        grid=(p // t,),
        in_specs=[pl.BlockSpec((t, ROW_W), by_block),
                  pl.BlockSpec((1, D_MODEL), lambda i, *_: (0, 0)),
                  pl.BlockSpec((E_PER_GROUP, D_MODEL, D_EXPERT), by_group),
                  pl.BlockSpec((E_PER_GROUP, D_MODEL, D_EXPERT), by_group),
                  pl.BlockSpec((E_PER_GROUP, D_EXPERT, D_MODEL), by_group)],
        out_specs=pl.BlockSpec((t, D_MODEL), by_block),
    )
    return pl.pallas_call(
        _moe_kernel,
        grid_spec=grid_spec,
        out_shape=jax.ShapeDtypeStruct((p, D_MODEL), F32),
        compiler_params=pltpu.CompilerParams(dimension_semantics=("arbitrary",),
                                             vmem_limit_bytes=VMEM_LIMIT),
        name="moe_experts",
    )(tile_group, tile_valid, xs, g2, wg, wu, wd)


def _gather_kernel(dest_ref, ys_ref, o_ref, sem):
    i = pl.program_id(0)
    tm = o_ref.shape[0]
    base = i * tm

    def start(r, carry):
        _row_copy(ys_ref, dest_ref[base + r], o_ref, r, sem).start()
        return carry

    lax.fori_loop(0, tm, start, 0, unroll=ISSUE_UNROLL)
    pltpu.make_async_copy(ys_ref.at[pl.ds(0, tm)], o_ref, sem).wait()


def _gather_call(dest, ys, n):
    tm = TM_PERM
    grid_spec = pltpu.PrefetchScalarGridSpec(
        num_scalar_prefetch=1,
        grid=(n // tm,),
        in_specs=[pl.BlockSpec(memory_space=pl.ANY)],
        out_specs=pl.BlockSpec((tm, D_MODEL), lambda i, *_: (i, 0)),
        scratch_shapes=[pltpu.SemaphoreType.DMA(())],
    )
    return pl.pallas_call(
        _gather_kernel,
        grid_spec=grid_spec,
        out_shape=jax.ShapeDtypeStruct((n, D_MODEL), F32),
        compiler_params=pltpu.CompilerParams(dimension_semantics=("arbitrary",),
                                             vmem_limit_bytes=VMEM_LIMIT),
        name="row_gather",
    )(dest, ys)


def _router_rows(w_rg, b_rg, w_re, b_re):
    w = jnp.zeros((ROUTE_ROWS, D_MODEL), F32)
    w = w.at[0:N_GROUPS].set(w_rg.T).at[SUBLANES:SUBLANES + N_EXPERTS].set(w_re.T)
    b = jnp.zeros((ROUTE_ROWS, 1), F32)
    b = b.at[0:N_GROUPS, 0].set(b_rg).at[SUBLANES:SUBLANES + N_EXPERTS, 0].set(b_re)
    hi = w.astype(BF16)
    lo = (w - hi.astype(F32)).astype(BF16)
    return hi, lo, b


def _layer(x2d, batch, seq_len, norm_mix_g, w_in, q_norm_g, k_norm_g, conv_w, w_sb_branch,
           w_conv_branch, w_out, norm_ffn_g, w_router_group, b_router_group, w_router_expert,
           b_router_expert, w_gate_e, w_up_e, w_down_e):
    n = x2d.shape[0]
    lane = jnp.arange(WIDTH)
    head_mean = jnp.where(lane[:, None] // HEAD_DIM == lane[None, :] // HEAD_DIM,
                          1.0 / HEAD_DIM, 0.0).astype(BF16)
    qg = (jnp.tile(q_norm_g, N_HEADS) * HEAD_DIM ** -0.5)[None, :]
    kg = jnp.tile(k_norm_g, N_HEADS)[None, :]
    q, k, v, y, sa, sb = _inproj_call(x2d, norm_mix_g[None, :], w_in.astype(BF16), qg, kg,
                                      head_mean, conv_w, seq_len)

    pos = jnp.arange(TK)
    tri = (pos[:, None] >= pos[None, :]).astype(BF16)
    attn = _attn_call(q, k, v, tri, batch, seq_len)

    wrh, wrl, br = _router_rows(w_router_group, b_router_group, w_router_expert, b_router_expert)
    tpos = jnp.arange(TM_PROJ)
    upper = (tpos[:, None] < tpos[None, :]).astype(BF16)
    x1e, route, cnt = _merge_call(attn, y, sa, sb, x2d, w_sb_branch.astype(BF16),
                                  w_conv_branch.astype(BF16), w_out.astype(BF16),
                                  norm_ffn_g[None, :], wrh, wrl, br, upper)

    t = TM_MOE
    grp = route[0].astype(jnp.int32)
    rank = route[1].astype(jnp.int32)
    counts = cnt[0:N_GROUPS, 0].astype(jnp.int32)
    padded = ((counts + t - 1) // t) * t
    ends = jnp.cumsum(padded)
    offs = ends - padded
    dest = rank
    for g in range(N_GROUPS):
        dest = dest + jnp.where(grp == g, offs[g], 0)
    n_tiles = n // t + N_GROUPS
    tile_start = jnp.arange(n_tiles, dtype=jnp.int32) * t
    tile_valid = (tile_start < ends[-1]).astype(jnp.int32)
    tile_group = jnp.zeros_like(tile_start)
    for g in range(N_GROUPS - 1):
        tile_group = tile_group + (jnp.minimum(tile_start, ends[-1] - t) >= ends[g]).astype(jnp.int32)
    p_rows = n_tiles * t
    fill_start = jnp.concatenate([offs + counts, ends[-1:]])
    fill_n = jnp.concatenate([padded - counts, p_rows - ends[-1:]])

    xs = _scatter_call(dest, fill_start, fill_n, x1e, p_rows)
    ys = _moe_call(tile_group, tile_valid, xs, norm_ffn_g[None, :],
                   w_gate_e.astype(BF16), w_up_e.astype(BF16), w_down_e.astype(BF16))
    return _gather_call(dest, ys, n)


def kernel(x, norm_mix_g, w_in, q_norm_g, k_norm_g, conv_w, w_sb_branch, w_conv_branch, w_out,
           norm_ffn_g, w_router_group, b_router_group, w_router_expert, b_router_expert,
           w_gate_e, w_up_e, w_down_e):
    batch, seq_len, d = x.shape
    x2d = x.reshape(batch * seq_len, d)
    for l in range(norm_mix_g.shape[0]):
        x2d = _layer(x2d, batch, seq_len, norm_mix_g[l], w_in[l], q_norm_g[l], k_norm_g[l],
                     conv_w[l], w_sb_branch[l], w_conv_branch[l], w_out[l], norm_ffn_g[l],
                     w_router_group[l], b_router_group[l], w_router_expert[l], b_router_expert[l],
                     w_gate_e[l], w_up_e[l], w_down_e[l])
    return x2d.reshape(batch, seq_len, d)
```

```python
import functools

import jax
import jax.numpy as jnp
from jax import lax
from jax.experimental import pallas as pl
from jax.experimental.pallas import tpu as pltpu

F32 = jnp.float32
BF16 = jnp.bfloat16

D_MODEL = 1024
N_HEADS = 8
HEAD_DIM = 64
WIDTH = N_HEADS * HEAD_DIM
CONV_K = 3
N_GROUPS = 4
E_PER_GROUP = 8
N_EXPERTS = N_GROUPS * E_PER_GROUP
D_EXPERT = 256
EPS = 1e-6

LANES = 128
SUBLANES = 8
HEADS_PER_STEP = LANES // HEAD_DIM
N_PAIRS = N_HEADS // HEADS_PER_STEP

TM_PROJ = 512
TQ = 256
TK = 256
TM_MOE = 512
TM_PERM = 512
ROW_W = D_MODEL + LANES
ISSUE_UNROLL = 8
ROUTE_ROWS = 48
VMEM_LIMIT = 56 * 1024 * 1024

_NT = (((1,), (1,)), ((), ()))


def _dot(a, b):
    return jnp.dot(a, b, preferred_element_type=F32)


def _split_bf16(v):
    hi = v.astype(BF16)
    lo = (v - hi.astype(F32)).astype(BF16)
    return hi, lo


def _inproj_kernel(x_ref, g1_ref, w_ref, qg_ref, kg_ref, pm_ref, cw_ref,
                   q_ref, k_ref, v_ref, y_ref, sa_ref, sb_ref, cu_scr, *, tiles_per_seq):
    tm = x_ref.shape[0]
    x = x_ref[...]
    ms = jnp.mean(x * x, axis=-1, keepdims=True)
    h = (x * lax.rsqrt(ms + EPS) * g1_ref[...]).astype(BF16)

    def proj(lo, width):
        return _dot(h, w_ref[:, lo:lo + width])

    def head_norm(t, g_ref):
        hms = _dot((t * t).astype(BF16), pm_ref[...])
        return (t * lax.rsqrt(hms + EPS) * g_ref[...]).astype(BF16)

    q_ref[...] = head_norm(proj(0, WIDTH), qg_ref)
    k_ref[...] = head_norm(proj(WIDTH, WIDTH), kg_ref)
    v_ref[...] = proj(2 * WIDTH, WIDTH).astype(BF16)

    c_b = proj(3 * WIDTH, WIDTH)
    cu = proj(4 * WIDTH, WIDTH) * proj(5 * WIDTH, WIDTH)

    @pl.when(pl.program_id(0) % tiles_per_seq == 0)
    def _():
        cu_scr[0:SUBLANES, :] = jnp.zeros((SUBLANES, WIDTH), F32)

    cu_scr[SUBLANES:SUBLANES + tm, :] = cu
    cw = cw_ref[...]
    conv = (cw[0:1, :] * cu_scr[SUBLANES - 2:SUBLANES - 2 + tm, :]
            + cw[1:2, :] * cu_scr[SUBLANES - 1:SUBLANES - 1 + tm, :]
            + cw[2:3, :] * cu)
    y_ref[...] = (c_b * conv).astype(BF16)
    cu_scr[0:SUBLANES, :] = cu_scr[tm:tm + SUBLANES, :]

    sa_ref[...] = jax.nn.sigmoid(proj(6 * WIDTH, D_MODEL))
    sb_ref[...] = jax.nn.sigmoid(proj(6 * WIDTH + D_MODEL, D_MODEL))


def _inproj_call(x2d, g1, w_in, qg, kg, pm, conv_w, seq_len):
    n = x2d.shape[0]
    tm = TM_PROJ
    in_w = w_in.shape[1]
    const = lambda shape: pl.BlockSpec(shape, lambda i: (0, 0), pipeline_mode=pl.Buffered(1))
    rows = lambda width: pl.BlockSpec((tm, width), lambda i: (i, 0))
    return pl.pallas_call(
        functools.partial(_inproj_kernel, tiles_per_seq=seq_len // tm),
        grid=(n // tm,),
        in_specs=[rows(D_MODEL), const((1, D_MODEL)), const((D_MODEL, in_w)), const((1, WIDTH)),
                  const((1, WIDTH)), const((WIDTH, WIDTH)), const((CONV_K, WIDTH))],
        out_specs=[rows(WIDTH), rows(WIDTH), rows(WIDTH), rows(WIDTH), rows(D_MODEL), rows(D_MODEL)],
        out_shape=[jax.ShapeDtypeStruct((n, WIDTH), BF16)] * 4
                  + [jax.ShapeDtypeStruct((n, D_MODEL), F32)] * 2,
        scratch_shapes=[pltpu.VMEM((SUBLANES + tm, WIDTH), F32)],
        compiler_params=pltpu.CompilerParams(dimension_semantics=("arbitrary",),
                                             vmem_limit_bytes=VMEM_LIMIT),
        name="inproj",
    )(x2d, g1, w_in, qg, kg, pm, conv_w)


def _softplus(z):
    return jnp.maximum(z, 0.0) + jnp.log(1.0 + jnp.exp(-jnp.abs(z)))


def _attn_kernel(q_ref, k_ref, v_ref, tri_ref, o_ref, qm_scr, z_scr, zc_scr, suf_scr, a_scr,
                 acc_scr, c_scr):
    i = pl.program_id(1)
    tq = q_ref.shape[0]
    lane = lax.broadcasted_iota(jnp.int32, (tq, LANES), 1)
    left = lane < HEAD_DIM
    for p in range(N_PAIRS):
        q2 = q_ref[:, p * LANES:(p + 1) * LANES]
        zero = jnp.zeros_like(q2)
        qm_scr[2 * p] = jnp.where(left, q2, zero)
        qm_scr[2 * p + 1] = jnp.where(left, zero, q2)
    acc_scr[...] = jnp.zeros_like(acc_scr)
    c_scr[...] = jnp.zeros_like(c_scr)

    def kv_rows(j):
        return pl.ds(pl.multiple_of(j * TK, TK), TK)

    def pair_lanes(s):
        p = s // HEADS_PER_STEP
        return slice(p * LANES, (p + 1) * LANES)

    def scores(s, j):
        z_scr[s] = lax.dot_general(qm_scr[s], k_ref[kv_rows(j), pair_lanes(s)], _NT,
                                   preferred_element_type=F32)

    def suffix_sums(s, mask):
        z = z_scr[s]
        zc_scr[s] = z - c_scr[s]
        sp = _softplus(z.astype(BF16))
        if mask is not None:
            sp = jnp.where(mask, sp, jnp.zeros_like(sp))
        suf_scr[s] = _dot(sp, tri_ref[...])

    def weights(s, mask):
        suffix = suf_scr[s]
        a = jnp.exp(zc_scr[s] - suffix)
        if mask is not None:
            a = jnp.where(mask, a, 0.0)
        a_scr[s] = a.astype(BF16)
        c_scr[s] = c_scr[s] + suffix[:, 0:1]

    def accumulate(s, j):
        acc_scr[s] = acc_scr[s] + _dot(a_scr[s], v_ref[kv_rows(j), pair_lanes(s)])

    row = lax.broadcasted_iota(jnp.int32, (tq, TK), 0)
    col = lax.broadcasted_iota(jnp.int32, (tq, TK), 1)
    causal = col < row
    nxt = jnp.maximum(i - 1, 0)
    for s in range(N_HEADS):
        scores(s, i)
    for s in range(N_HEADS):
        suffix_sums(s, causal)
        scores(s, nxt)
    for s in range(N_HEADS):
        weights(s, causal)

    def body(t, carry):
        j = i - t
        nxt = jnp.maximum(j - 1, 0)
        for s in range(N_HEADS):
            accumulate(s, j + 1)
            suffix_sums(s, None)
            scores(s, nxt)
        for s in range(N_HEADS):
            weights(s, None)
        return carry

    lax.fori_loop(1, i + 1, body, 0)
    for s in range(N_HEADS):
        accumulate(s, 0)
    for p in range(N_PAIRS):
        o_ref[:, p * LANES:(p + 1) * LANES] = jnp.where(
            left, acc_scr[2 * p], acc_scr[2 * p + 1]).astype(o_ref.dtype)


def _attn_call(q, k, v, tri, batch, seq_len):
    assert TQ == TK
    nq = seq_len // TQ
    qspec = pl.BlockSpec((TQ, WIDTH), lambda b, i: (b * nq + i, 0))
    kvspec = pl.BlockSpec((seq_len, WIDTH), lambda b, i: (b, 0))
    return pl.pallas_call(
        _attn_kernel,
        grid=(batch, nq),
        in_specs=[qspec, kvspec, kvspec, pl.BlockSpec((TK, TK), lambda b, i: (0, 0))],
        out_specs=qspec,
        out_shape=jax.ShapeDtypeStruct(q.shape, BF16),
        scratch_shapes=[pltpu.VMEM((N_HEADS, TQ, LANES), BF16),
                        pltpu.VMEM((N_HEADS, TQ, TK), F32),
                        pltpu.VMEM((N_HEADS, TQ, TK), F32),
                        pltpu.VMEM((N_HEADS, TQ, TK), F32),
                        pltpu.VMEM((N_HEADS, TQ, TK), BF16),
                        pltpu.VMEM((N_HEADS, TQ, LANES), F32),
                        pltpu.VMEM((N_HEADS, TQ, 1), F32)],
        compiler_params=pltpu.CompilerParams(dimension_semantics=("arbitrary", "arbitrary"),
                                             vmem_limit_bytes=VMEM_LIMIT),
        name="sb_attn",
    )(q, k, v, tri)


def _merge_kernel(attn_ref, y_ref, sa_ref, sb_ref, x_ref, wsb_ref, wcv_ref, wout_ref, g2_ref,
                  wrh_ref, wrl_ref, br_ref, upper_ref, x1e_ref, route_ref, cnt_ref, cnt_scr):
    tm = x_ref.shape[0]
    branch_a = _dot(attn_ref[...], wsb_ref[...])
    branch_b = _dot(y_ref[...], wcv_ref[...])
    merged = (sa_ref[...] * branch_a + sb_ref[...] * branch_b).astype(BF16)
    x1 = x_ref[...] + _dot(merged, wout_ref[...])
    x1e_ref[:, 0:D_MODEL] = x1
    ms = jnp.mean(x1 * x1, axis=-1, keepdims=True)
    h2 = x1 * lax.rsqrt(ms + EPS) * g2_ref[...]

    hh, hl = _split_bf16(h2)
    nt = functools.partial(lax.dot_general, dimension_numbers=_NT, preferred_element_type=F32)
    logits = nt(wrh_ref[...], hh) + nt(wrh_ref[...], hl) + nt(wrl_ref[...], hh) + br_ref[...]

    rowi = lax.broadcasted_iota(jnp.int32, (SUBLANES, tm), 0).astype(F32)
    neg = jnp.float32(-jnp.inf)
    big = jnp.float32(SUBLANES)

    def first_argmax(vals):
        top = jnp.max(vals, axis=0, keepdims=True)
        idx = jnp.min(jnp.where(vals == top, rowi, big), axis=0, keepdims=True)
        return top, idx

    gl = jnp.where(rowi < N_GROUPS, logits[0:SUBLANES, :], neg)
    gmax, gidx = first_argmax(gl)
    g_w = 1.0 / jnp.sum(jnp.exp(gl - gmax), axis=0, keepdims=True)
    el = logits[SUBLANES:2 * SUBLANES, :]
    for g in range(1, N_GROUPS):
        el = jnp.where(gidx == g, logits[(g + 1) * SUBLANES:(g + 2) * SUBLANES, :], el)
    t1, i1 = first_argmax(el)
    t2, i2 = first_argmax(jnp.where(rowi == i1, neg, el))
    e = jnp.exp(t2 - t1)
    w1 = g_w / (1.0 + e)
    w2 = g_w * e / (1.0 + e)

    cw_t = jnp.where(rowi == i1, w1, 0.0) + jnp.where(rowi == i2, w2, 0.0)
    cw_t = jnp.concatenate([cw_t, jnp.zeros((LANES - SUBLANES, tm), F32)], axis=0)
    x1e_ref[:, D_MODEL:D_MODEL + LANES] = cw_t.T

    @pl.when(pl.program_id(0) == 0)
    def _():
        cnt_scr[...] = jnp.zeros_like(cnt_scr)

    onehot = jnp.where(rowi == gidx, 1.0, 0.0)
    before = _dot(onehot.astype(BF16), upper_ref[...]) + cnt_scr[:, 0:1]
    rank = jnp.sum(onehot * before, axis=0, keepdims=True)
    cnt_scr[...] = cnt_scr[...] + jnp.sum(onehot, axis=1, keepdims=True)
    cnt_ref[...] = cnt_scr[...]
    route_ref[...] = jnp.where(rowi == 0, gidx, jnp.where(rowi == 1, rank, 0.0))


def _merge_call(attn, y, sa, sb, x2d, wsb, wcv, wout, g2, wrh, wrl, br, upper):
    n = x2d.shape[0]
    tm = TM_PROJ
    rows = lambda width: pl.BlockSpec((tm, width), lambda i: (i, 0))
    const = lambda shape: pl.BlockSpec(shape, lambda i: (0, 0), pipeline_mode=pl.Buffered(1))
    return pl.pallas_call(
        _merge_kernel,
        grid=(n // tm,),
        in_specs=[rows(WIDTH), rows(WIDTH), rows(D_MODEL), rows(D_MODEL), rows(D_MODEL),
                  const((WIDTH, D_MODEL)), const((WIDTH, D_MODEL)), const((D_MODEL, D_MODEL)),
                  const((1, D_MODEL)), const((ROUTE_ROWS, D_MODEL)), const((ROUTE_ROWS, D_MODEL)),
                  const((ROUTE_ROWS, 1)), const((tm, tm))],
        out_specs=[rows(ROW_W), pl.BlockSpec((SUBLANES, tm), lambda i: (0, i)),
                   pl.BlockSpec((SUBLANES, LANES), lambda i: (0, 0))],
        out_shape=[jax.ShapeDtypeStruct((n, ROW_W), F32), jax.ShapeDtypeStruct((SUBLANES, n), F32),
                   jax.ShapeDtypeStruct((SUBLANES, LANES), F32)],
        scratch_shapes=[pltpu.VMEM((SUBLANES, LANES), F32)],
        compiler_params=pltpu.CompilerParams(dimension_semantics=("arbitrary",),
                                             vmem_limit_bytes=VMEM_LIMIT),
        name="merge_router",
    )(attn, y, sa, sb, x2d, wsb, wcv, wout, g2, wrh, wrl, br, upper)


def _row_copy(src_ref, src_row, dst_ref, dst_row, sem):
    return pltpu.make_async_copy(src_ref.at[pl.ds(src_row, 1)], dst_ref.at[pl.ds(dst_row, 1)], sem)


def _scatter_kernel(dest_ref, fill_start_ref, fill_n_ref, x_ref, xs_ref, zero_scr, sem):
    i = pl.program_id(0)
    tm = x_ref.shape[0]
    base = i * tm

    def start(r, carry):
        _row_copy(x_ref, r, xs_ref, dest_ref[base + r], sem).start()
        return carry

    lax.fori_loop(0, tm, start, 0, unroll=ISSUE_UNROLL)
    pltpu.make_async_copy(x_ref, xs_ref.at[pl.ds(0, tm)], sem).wait()

    @pl.when(i == pl.num_programs(0) - 1)
    def _():
        zero_scr[...] = jnp.zeros_like(zero_scr)
        for g in range(N_GROUPS + 1):
            first = fill_start_ref[g]

            def start_fill(r, carry):
                _row_copy(zero_scr, 0, xs_ref, first + r, sem).start()
                return carry

            def wait_fill(r, carry):
                _row_copy(zero_scr, 0, xs_ref, 0, sem).wait()
                return carry

            lax.fori_loop(0, fill_n_ref[g], start_fill, 0)
            lax.fori_loop(0, fill_n_ref[g], wait_fill, 0)


def _scatter_call(dest, fill_start, fill_n, x1e, p_rows):
    n = x1e.shape[0]
    tm = TM_PERM
    grid_spec = pltpu.PrefetchScalarGridSpec(
        num_scalar_prefetch=3,
        grid=(n // tm,),
        in_specs=[pl.BlockSpec((tm, ROW_W), lambda i, *_: (i, 0))],
        out_specs=pl.BlockSpec(memory_space=pl.ANY),
        scratch_shapes=[pltpu.VMEM((SUBLANES, ROW_W), F32), pltpu.SemaphoreType.DMA(())],
    )
    return pl.pallas_call(
        _scatter_kernel,
        grid_spec=grid_spec,
        out_shape=jax.ShapeDtypeStruct((p_rows, ROW_W), F32),
        compiler_params=pltpu.CompilerParams(dimension_semantics=("arbitrary",),
                                             vmem_limit_bytes=VMEM_LIMIT),
        name="row_scatter",
    )(dest, fill_start, fill_n, x1e)


def _moe_kernel(tg_ref, tv_ref, xs_ref, g2_ref, wg_ref, wu_ref, wd_ref, o_ref):
    i = pl.program_id(0)

    @pl.when(tv_ref[i] == 0)
    def _():
        o_ref[...] = jnp.zeros_like(o_ref)

    @pl.when(tv_ref[i] > 0)
    def _():
        x1 = xs_ref[:, 0:D_MODEL]
        cw = xs_ref[:, D_MODEL:D_MODEL + LANES]
        ms = jnp.mean(x1 * x1, axis=-1, keepdims=True)
        h2 = (x1 * lax.rsqrt(ms + EPS) * g2_ref[...]).astype(BF16)
        acc = x1
        for e in range(E_PER_GROUP):
            gate = _dot(h2, wg_ref[e])
            up = _dot(h2, wu_ref[e])
            act = (gate * jax.nn.sigmoid(gate)) * up * cw[:, e:e + 1]
            acc = acc + _dot(act.astype(BF16), wd_ref[e])
        o_ref[...] = acc


def _moe_call(tile_group, tile_valid, xs, g2, wg, wu, wd):
    p = xs.shape[0]
    t = TM_MOE
    by_block = lambda i, tg, tv: (i, 0)
    by_group = lambda i, tg, tv: (tg[i], 0, 0)
    once = pl.Buffered(1)
    grid_spec = pltpu.PrefetchScalarGridSpec(
        num_scalar_prefetch=2,
        grid=(p // t,),
        in_specs=[pl.BlockSpec((t, ROW_W), by_block),
                  pl.BlockSpec((1, D_MODEL), lambda i, *_: (0, 0)),
                  pl.BlockSpec((E_PER_GROUP, D_MODEL, D_EXPERT), by_group, pipeline_mode=once),
                  pl.BlockSpec((E_PER_GROUP, D_MODEL, D_EXPERT), by_group, pipeline_mode=once),
                  pl.BlockSpec((E_PER_GROUP, D_EXPERT, D_MODEL), by_group, pipeline_mode=once)],
        out_specs=pl.BlockSpec((t, D_MODEL), by_block),
    )
    return pl.pallas_call(
        _moe_kernel,
        grid_spec=grid_spec,
        out_shape=jax.ShapeDtypeStruct((p, D_MODEL), F32),
        compiler_params=pltpu.CompilerParams(dimension_semantics=("arbitrary",),
                                             vmem_limit_bytes=VMEM_LIMIT),
        name="moe_experts",
    )(tile_group, tile_valid, xs, g2, wg, wu, wd)


def _gather_kernel(dest_ref, ys_ref, o_ref, sem):
    i = pl.program_id(0)
    tm = o_ref.shape[0]
    base = i * tm

    def start(r, carry):
        _row_copy(ys_ref, dest_ref[base + r], o_ref, r, sem).start()
        return carry

    lax.fori_loop(0, tm, start, 0, unroll=ISSUE_UNROLL)
    pltpu.make_async_copy(ys_ref.at[pl.ds(0, tm)], o_ref, sem).wait()


def _gather_call(dest, ys, n):
    tm = TM_PERM
    grid_spec = pltpu.PrefetchScalarGridSpec(
        num_scalar_prefetch=1,
        grid=(n // tm,),
        in_specs=[pl.BlockSpec(memory_space=pl.ANY)],
        out_specs=pl.BlockSpec((tm, D_MODEL), lambda i, *_: (i, 0)),
        scratch_shapes=[pltpu.SemaphoreType.DMA(())],
    )
    return pl.pallas_call(
        _gather_kernel,
        grid_spec=grid_spec,
        out_shape=jax.ShapeDtypeStruct((n, D_MODEL), F32),
        compiler_params=pltpu.CompilerParams(dimension_semantics=("arbitrary",),
                                             vmem_limit_bytes=VMEM_LIMIT),
        name="row_gather",
    )(dest, ys)


def _router_rows(w_rg, b_rg, w_re, b_re):
    w = jnp.zeros((ROUTE_ROWS, D_MODEL), F32)
    w = w.at[0:N_GROUPS].set(w_rg.T).at[SUBLANES:SUBLANES + N_EXPERTS].set(w_re.T)
    b = jnp.zeros((ROUTE_ROWS, 1), F32)
    b = b.at[0:N_GROUPS, 0].set(b_rg).at[SUBLANES:SUBLANES + N_EXPERTS, 0].set(b_re)
    hi = w.astype(BF16)
    lo = (w - hi.astype(F32)).astype(BF16)
    return hi, lo, b


def _layer(x2d, batch, seq_len, norm_mix_g, w_in, q_norm_g, k_norm_g, conv_w, w_sb_branch,
           w_conv_branch, w_out, norm_ffn_g, w_router_group, b_router_group, w_router_expert,
           b_router_expert, w_gate_e, w_up_e, w_down_e):
    n = x2d.shape[0]
    lane = jnp.arange(WIDTH)
    head_mean = jnp.where(lane[:, None] // HEAD_DIM == lane[None, :] // HEAD_DIM,
                          1.0 / HEAD_DIM, 0.0).astype(BF16)
    qg = (jnp.tile(q_norm_g, N_HEADS) * HEAD_DIM ** -0.5)[None, :]
    kg = jnp.tile(k_norm_g, N_HEADS)[None, :]
    q, k, v, y, sa, sb = _inproj_call(x2d, norm_mix_g[None, :], w_in.astype(BF16), qg, kg,
                                      head_mean, conv_w, seq_len)

    pos = jnp.arange(TK)
    tri = (pos[:, None] >= pos[None, :]).astype(BF16)
    attn = _attn_call(q, k, v, tri, batch, seq_len)

    wrh, wrl, br = _router_rows(w_router_group, b_router_group, w_router_expert, b_router_expert)
    tpos = jnp.arange(TM_PROJ)
    upper = (tpos[:, None] < tpos[None, :]).astype(BF16)
    x1e, route, cnt = _merge_call(attn, y, sa, sb, x2d, w_sb_branch.astype(BF16),
                                  w_conv_branch.astype(BF16), w_out.astype(BF16),
                                  norm_ffn_g[None, :], wrh, wrl, br, upper)

    t = TM_MOE
    grp = route[0].astype(jnp.int32)
    rank = route[1].astype(jnp.int32)
    counts = cnt[0:N_GROUPS, 0].astype(jnp.int32)
    padded = ((counts + t - 1) // t) * t
    ends = jnp.cumsum(padded)
    offs = ends - padded
    dest = rank
    for g in range(N_GROUPS):
        dest = dest + jnp.where(grp == g, offs[g], 0)
    n_tiles = n // t + N_GROUPS
    tile_start = jnp.arange(n_tiles, dtype=jnp.int32) * t
    tile_valid = (tile_start < ends[-1]).astype(jnp.int32)
    tile_group = jnp.zeros_like(tile_start)
    for g in range(N_GROUPS - 1):
        tile_group = tile_group + (jnp.minimum(tile_start, ends[-1] - t) >= ends[g]).astype(jnp.int32)
    p_rows = n_tiles * t
    fill_start = jnp.concatenate([offs + counts, ends[-1:]])
    fill_n = jnp.concatenate([padded - counts, p_rows - ends[-1:]])

    xs = _scatter_call(dest, fill_start, fill_n, x1e, p_rows)
    ys = _moe_call(tile_group, tile_valid, xs, norm_ffn_g[None, :],
                   w_gate_e.astype(BF16), w_up_e.astype(BF16), w_down_e.astype(BF16))
    return _gather_call(dest, ys, n)


def kernel(x, norm_mix_g, w_in, q_norm_g, k_norm_g, conv_w, w_sb_branch, w_conv_branch, w_out,
           norm_ffn_g, w_router_group, b_router_group, w_router_expert, b_router_expert,
           w_gate_e, w_up_e, w_down_e):
    batch, seq_len, d = x.shape
    x2d = x.reshape(batch * seq_len, d)
    for l in range(norm_mix_g.shape[0]):
        x2d = _layer(x2d, batch, seq_len, norm_mix_g[l], w_in[l], q_norm_g[l], k_norm_g[l],
                     conv_w[l], w_sb_branch[l], w_conv_branch[l], w_out[l], norm_ffn_g[l],
                     w_router_group[l], b_router_group[l], w_router_expert[l], b_router_expert[l],
                     w_gate_e[l], w_up_e[l], w_down_e[l])
    return x2d.reshape(batch, seq_len, d)
```

```python
import functools

import jax
import jax.numpy as jnp
from jax import lax
from jax.experimental import pallas as pl
from jax.experimental.pallas import tpu as pltpu

F32 = jnp.float32
BF16 = jnp.bfloat16

D_MODEL = 1024
N_HEADS = 8
HEAD_DIM = 64
WIDTH = N_HEADS * HEAD_DIM
CONV_K = 3
N_GROUPS = 4
E_PER_GROUP = 8
N_EXPERTS = N_GROUPS * E_PER_GROUP
D_EXPERT = 256
EPS = 1e-6

LANES = 128
SUBLANES = 8
HEADS_PER_STEP = LANES // HEAD_DIM
N_PAIRS = N_HEADS // HEADS_PER_STEP

TM_PROJ = 512
TQ = 256
TK = 256
TM_MOE = 512
TM_PERM = 512
ROW_W = D_MODEL + LANES
ISSUE_UNROLL = 8
ROUTE_ROWS = 48
VMEM_LIMIT = 56 * 1024 * 1024

_NT = (((1,), (1,)), ((), ()))


def _dot(a, b):
    return jnp.dot(a, b, preferred_element_type=F32)


def _split_bf16(v):
    hi = v.astype(BF16)
    lo = (v - hi.astype(F32)).astype(BF16)
    return hi, lo


def _inproj_kernel(x_ref, g1_ref, w_ref, qg_ref, kg_ref, pm_ref, cw_ref,
                   q_ref, k_ref, v_ref, y_ref, sa_ref, sb_ref, cu_scr, *, tiles_per_seq):
    tm = x_ref.shape[0]
    x = x_ref[...]
    ms = jnp.mean(x * x, axis=-1, keepdims=True)
    h = (x * lax.rsqrt(ms + EPS) * g1_ref[...]).astype(BF16)

    def proj(lo, width):
        return _dot(h, w_ref[:, lo:lo + width])

    def head_norm(t, g_ref):
        hms = _dot((t * t).astype(BF16), pm_ref[...])
        return (t * lax.rsqrt(hms + EPS) * g_ref[...]).astype(BF16)

    q_ref[...] = head_norm(proj(0, WIDTH), qg_ref)
    k_ref[...] = head_norm(proj(WIDTH, WIDTH), kg_ref)
    v_ref[...] = proj(2 * WIDTH, WIDTH).astype(BF16)

    c_b = proj(3 * WIDTH, WIDTH)
    cu = proj(4 * WIDTH, WIDTH) * proj(5 * WIDTH, WIDTH)

    @pl.when(pl.program_id(0) % tiles_per_seq == 0)
    def _():
        cu_scr[0:SUBLANES, :] = jnp.zeros((SUBLANES, WIDTH), F32)

    cu_scr[SUBLANES:SUBLANES + tm, :] = cu
    cw = cw_ref[...]
    conv = (cw[0:1, :] * cu_scr[SUBLANES - 2:SUBLANES - 2 + tm, :]
            + cw[1:2, :] * cu_scr[SUBLANES - 1:SUBLANES - 1 + tm, :]
            + cw[2:3, :] * cu)
    y_ref[...] = (c_b * conv).astype(BF16)
    cu_scr[0:SUBLANES, :] = cu_scr[tm:tm + SUBLANES, :]

    sa_ref[...] = jax.nn.sigmoid(proj(6 * WIDTH, D_MODEL))
    sb_ref[...] = jax.nn.sigmoid(proj(6 * WIDTH + D_MODEL, D_MODEL))


def _inproj_call(x2d, g1, w_in, qg, kg, pm, conv_w, seq_len):
    n = x2d.shape[0]
    tm = TM_PROJ
    in_w = w_in.shape[1]
    const = lambda shape: pl.BlockSpec(shape, lambda i: (0, 0), pipeline_mode=pl.Buffered(1))
    rows = lambda width: pl.BlockSpec((tm, width), lambda i: (i, 0))
    return pl.pallas_call(
        functools.partial(_inproj_kernel, tiles_per_seq=seq_len // tm),
        grid=(n // tm,),
        in_specs=[rows(D_MODEL), const((1, D_MODEL)), const((D_MODEL, in_w)), const((1, WIDTH)),
                  const((1, WIDTH)), const((WIDTH, WIDTH)), const((CONV_K, WIDTH))],
        out_specs=[rows(WIDTH), rows(WIDTH), rows(WIDTH), rows(WIDTH), rows(D_MODEL), rows(D_MODEL)],
        out_shape=[jax.ShapeDtypeStruct((n, WIDTH), BF16)] * 4
                  + [jax.ShapeDtypeStruct((n, D_MODEL), F32)] * 2,
        scratch_shapes=[pltpu.VMEM((SUBLANES + tm, WIDTH), F32)],
        compiler_params=pltpu.CompilerParams(dimension_semantics=("arbitrary",),
                                             vmem_limit_bytes=VMEM_LIMIT),
        name="inproj",
    )(x2d, g1, w_in, qg, kg, pm, conv_w)


def _softplus(z):
    return jnp.maximum(z, 0.0) + jnp.log(1.0 + jnp.exp(-jnp.abs(z)))


def _attn_kernel(q_ref, k_ref, v_ref, tri_ref, wg_ref, wu_ref, wd_ref, o_ref, wg16_ref, wu16_ref,
                 wd16_ref, qm_scr, z_scr, zc_scr, suf_scr, a_scr, acc_scr, c_scr):
    i = pl.program_id(1)
    tq = q_ref.shape[0]
    lane = lax.broadcasted_iota(jnp.int32, (tq, LANES), 1)
    left = lane < HEAD_DIM
    for p in range(N_PAIRS):
        q2 = q_ref[:, p * LANES:(p + 1) * LANES]
        zero = jnp.zeros_like(q2)
        qm_scr[2 * p] = jnp.where(left, q2, zero)
        qm_scr[2 * p + 1] = jnp.where(left, zero, q2)
    acc_scr[...] = jnp.zeros_like(acc_scr)
    c_scr[...] = jnp.zeros_like(c_scr)

    def kv_rows(j):
        return pl.ds(pl.multiple_of(j * TK, TK), TK)

    def pair_lanes(s):
        p = s // HEADS_PER_STEP
        return slice(p * LANES, (p + 1) * LANES)

    def scores(s, j):
        z_scr[s] = lax.dot_general(qm_scr[s], k_ref[kv_rows(j), pair_lanes(s)], _NT,
                                   preferred_element_type=F32)

    def suffix_sums(s, mask):
        z = z_scr[s]
        zc_scr[s] = z - c_scr[s]
        sp = _softplus(z.astype(BF16))
        if mask is not None:
            sp = jnp.where(mask, sp, jnp.zeros_like(sp))
        suf_scr[s] = _dot(sp, tri_ref[...])

    def weights(s, mask):
        suffix = suf_scr[s]
        a = jnp.exp(zc_scr[s] - suffix)
        if mask is not None:
            a = jnp.where(mask, a, 0.0)
        a_scr[s] = a.astype(BF16)
        c_scr[s] = c_scr[s] + suffix[:, 0:1]

    def accumulate(s, j):
        acc_scr[s] = acc_scr[s] + _dot(a_scr[s], v_ref[kv_rows(j), pair_lanes(s)])

    row = lax.broadcasted_iota(jnp.int32, (tq, TK), 0)
    col = lax.broadcasted_iota(jnp.int32, (tq, TK), 1)
    causal = col < row
    nxt = jnp.maximum(i - 1, 0)
    for s in range(N_HEADS):
        scores(s, i)
    for s in range(N_HEADS):
        suffix_sums(s, causal)
        scores(s, nxt)
    for s in range(N_HEADS):
        weights(s, causal)

    def body(t, carry):
        j = i - t
        nxt = jnp.maximum(j - 1, 0)
        for s in range(N_HEADS):
            accumulate(s, j + 1)
            suffix_sums(s, None)
            scores(s, nxt)
        for s in range(N_HEADS):
            weights(s, None)
        return carry

    lax.fori_loop(1, i + 1, body, 0)
    for s in range(N_HEADS):
        accumulate(s, 0)
    for p in range(N_PAIRS):
        o_ref[:, p * LANES:(p + 1) * LANES] = jnp.where(
            left, acc_scr[2 * p], acc_scr[2 * p + 1]).astype(o_ref.dtype)

    wg16_ref[...] = wg_ref[...].astype(BF16)
    wu16_ref[...] = wu_ref[...].astype(BF16)
    wd16_ref[...] = wd_ref[...].astype(BF16)


def _attn_call(q, k, v, tri, expert_weights, batch, seq_len):
    assert TQ == TK
    nq = seq_len // TQ
    steps = batch * nq
    qspec = pl.BlockSpec((TQ, WIDTH), lambda b, i: (b * nq + i, 0))
    kvspec = pl.BlockSpec((seq_len, WIDTH), lambda b, i: (b, 0))
    slabs = [w.reshape(steps, -1, D_MODEL) for w in expert_weights]
    wspec = pl.BlockSpec((1,) + slabs[0].shape[1:], lambda b, i: (b * nq + i, 0, 0))
    attn, *cast = pl.pallas_call(
        _attn_kernel,
        grid=(batch, nq),
        in_specs=[qspec, kvspec, kvspec, pl.BlockSpec((TK, TK), lambda b, i: (0, 0))] + [wspec] * 3,
        out_specs=[qspec] + [wspec] * 3,
        out_shape=[jax.ShapeDtypeStruct(q.shape, BF16)]
                  + [jax.ShapeDtypeStruct(s.shape, BF16) for s in slabs],
        scratch_shapes=[pltpu.VMEM((N_HEADS, TQ, LANES), BF16),
                        pltpu.VMEM((N_HEADS, TQ, TK), F32),
                        pltpu.VMEM((N_HEADS, TQ, TK), F32),
                        pltpu.VMEM((N_HEADS, TQ, TK), F32),
                        pltpu.VMEM((N_HEADS, TQ, TK), BF16),
                        pltpu.VMEM((N_HEADS, TQ, LANES), F32),
                        pltpu.VMEM((N_HEADS, TQ, 1), F32)],
        compiler_params=pltpu.CompilerParams(dimension_semantics=("arbitrary", "arbitrary"),
                                             vmem_limit_bytes=VMEM_LIMIT),
        name="sb_attn",
    )(q, k, v, tri, *slabs)
    return attn, [c.reshape(w.shape) for c, w in zip(cast, expert_weights)]


def _merge_kernel(attn_ref, y_ref, sa_ref, sb_ref, x_ref, wsb_ref, wcv_ref, wout_ref, g2_ref,
                  wrh_ref, wrl_ref, br_ref, upper_ref, x1e_ref, route_ref, cnt_ref, cnt_scr):
    tm = x_ref.shape[0]
    branch_a = _dot(attn_ref[...], wsb_ref[...])
    branch_b = _dot(y_ref[...], wcv_ref[...])
    merged = (sa_ref[...] * branch_a + sb_ref[...] * branch_b).astype(BF16)
    x1 = x_ref[...] + _dot(merged, wout_ref[...])
    x1e_ref[:, 0:D_MODEL] = x1
    ms = jnp.mean(x1 * x1, axis=-1, keepdims=True)
    h2 = x1 * lax.rsqrt(ms + EPS) * g2_ref[...]

    hh, hl = _split_bf16(h2)
    nt = functools.partial(lax.dot_general, dimension_numbers=_NT, preferred_element_type=F32)
    logits = nt(wrh_ref[...], hh) + nt(wrh_ref[...], hl) + nt(wrl_ref[...], hh) + br_ref[...]

    rowi = lax.broadcasted_iota(jnp.int32, (SUBLANES, tm), 0).astype(F32)
    neg = jnp.float32(-jnp.inf)
    big = jnp.float32(SUBLANES)

    def first_argmax(vals):
        top = jnp.max(vals, axis=0, keepdims=True)
        idx = jnp.min(jnp.where(vals == top, rowi, big), axis=0, keepdims=True)
        return top, idx

    gl = jnp.where(rowi < N_GROUPS, logits[0:SUBLANES, :], neg)
    gmax, gidx = first_argmax(gl)
    g_w = 1.0 / jnp.sum(jnp.exp(gl - gmax), axis=0, keepdims=True)
    el = logits[SUBLANES:2 * SUBLANES, :]
    for g in range(1, N_GROUPS):
        el = jnp.where(gidx == g, logits[(g + 1) * SUBLANES:(g + 2) * SUBLANES, :], el)
    t1, i1 = first_argmax(el)
    t2, i2 = first_argmax(jnp.where(rowi == i1, neg, el))
    e = jnp.exp(t2 - t1)
    w1 = g_w / (1.0 + e)
    w2 = g_w * e / (1.0 + e)

    cw_t = jnp.where(rowi == i1, w1, 0.0) + jnp.where(rowi == i2, w2, 0.0)
    cw_t = jnp.concatenate([cw_t, jnp.zeros((LANES - SUBLANES, tm), F32)], axis=0)
    x1e_ref[:, D_MODEL:D_MODEL + LANES] = cw_t.T

    @pl.when(pl.program_id(0) == 0)
    def _():
        cnt_scr[...] = jnp.zeros_like(cnt_scr)

    onehot = jnp.where(rowi == gidx, 1.0, 0.0)
    before = _dot(onehot.astype(BF16), upper_ref[...]) + cnt_scr[:, 0:1]
    rank = jnp.sum(onehot * before, axis=0, keepdims=True)
    cnt_scr[...] = cnt_scr[...] + jnp.sum(onehot, axis=1, keepdims=True)
    cnt_ref[...] = cnt_scr[...]
    route_ref[...] = jnp.where(rowi == 0, gidx, jnp.where(rowi == 1, rank, 0.0))


def _merge_call(attn, y, sa, sb, x2d, wsb, wcv, wout, g2, wrh, wrl, br, upper):
    n = x2d.shape[0]
    tm = TM_PROJ
    rows = lambda width: pl.BlockSpec((tm, width), lambda i: (i, 0))
    const = lambda shape: pl.BlockSpec(shape, lambda i: (0, 0), pipeline_mode=pl.Buffered(1))
    return pl.pallas_call(
        _merge_kernel,
        grid=(n // tm,),
        in_specs=[rows(WIDTH), rows(WIDTH), rows(D_MODEL), rows(D_MODEL), rows(D_MODEL),
                  const((WIDTH, D_MODEL)), const((WIDTH, D_MODEL)), const((D_MODEL, D_MODEL)),
                  const((1, D_MODEL)), const((ROUTE_ROWS, D_MODEL)), const((ROUTE_ROWS, D_MODEL)),
                  const((ROUTE_ROWS, 1)), const((tm, tm))],
        out_specs=[rows(ROW_W), pl.BlockSpec((SUBLANES, tm), lambda i: (0, i)),
                   pl.BlockSpec((SUBLANES, LANES), lambda i: (0, 0))],
        out_shape=[jax.ShapeDtypeStruct((n, ROW_W), F32), jax.ShapeDtypeStruct((SUBLANES, n), F32),
                   jax.ShapeDtypeStruct((SUBLANES, LANES), F32)],
        scratch_shapes=[pltpu.VMEM((SUBLANES, LANES), F32)],
        compiler_params=pltpu.CompilerParams(dimension_semantics=("arbitrary",),
                                             vmem_limit_bytes=VMEM_LIMIT),
        name="merge_router",
    )(attn, y, sa, sb, x2d, wsb, wcv, wout, g2, wrh, wrl, br, upper)


def _split_row(r):
    return lax.shift_right_logical(r, 3), lax.bitwise_and(r, SUBLANES - 1)


def _row_copy(src_ref, src_at, dst_ref, dst_at, sem):
    (sg, sr), (dg, dr) = src_at, dst_at
    return pltpu.make_async_copy(src_ref.at[sg, pl.ds(sr, 1)], dst_ref.at[dg, pl.ds(dr, 1)], sem)


def _permute_block(vmem_ref, hbm_ref, index_ref, gather, sem):
    groups = vmem_ref.shape[0]
    first = pl.program_id(0) * groups * SUBLANES

    def issue_group(k, carry):
        for u in range(SUBLANES):
            other = _split_row(index_ref[first + k * SUBLANES + u])
            if gather:
                _row_copy(hbm_ref, other, vmem_ref, (k, u), sem).start()
            else:
                _row_copy(vmem_ref, (k, u), hbm_ref, other, sem).start()
        return carry

    lax.fori_loop(0, groups, issue_group, 0)
    hbm_block = hbm_ref.at[pl.ds(0, groups)]
    if gather:
        pltpu.make_async_copy(hbm_block, vmem_ref, sem).wait()
    else:
        pltpu.make_async_copy(vmem_ref, hbm_block, sem).wait()


def _scatter_kernel(dest_ref, fill_start_ref, fill_n_ref, x_ref, xs_ref, zero_scr, sem):
    _permute_block(x_ref, xs_ref, dest_ref, False, sem)

    @pl.when(pl.program_id(0) == pl.num_programs(0) - 1)
    def _():
        zero_scr[...] = jnp.zeros_like(zero_scr)
        for g in range(N_GROUPS + 1):
            first = fill_start_ref[g]

            def start_fill(r, carry):
                _row_copy(zero_scr, (0, 0), xs_ref, _split_row(first + r), sem).start()
                return carry

            def wait_fill(r, carry):
                _row_copy(zero_scr, (0, 0), xs_ref, (0, 0), sem).wait()
                return carry

            lax.fori_loop(0, fill_n_ref[g], start_fill, 0)
            lax.fori_loop(0, fill_n_ref[g], wait_fill, 0)


def _row_groups(a):
    return a.reshape(a.shape[0] // SUBLANES, SUBLANES, a.shape[1])


def _scatter_call(dest, fill_start, fill_n, x1e, p_rows):
    n = x1e.shape[0]
    assert n % TM_PERM == 0 and p_rows % SUBLANES == 0
    grid_spec = pltpu.PrefetchScalarGridSpec(
        num_scalar_prefetch=3,
        grid=(n // TM_PERM,),
        in_specs=[pl.BlockSpec((TM_PERM // SUBLANES, SUBLANES, ROW_W), lambda i, *_: (i, 0, 0))],
        out_specs=pl.BlockSpec(memory_space=pl.ANY),
        scratch_shapes=[pltpu.VMEM((1, SUBLANES, ROW_W), F32), pltpu.SemaphoreType.DMA(())],
    )
    xs = pl.pallas_call(
        _scatter_kernel,
        grid_spec=grid_spec,
        out_shape=jax.ShapeDtypeStruct((p_rows // SUBLANES, SUBLANES, ROW_W), F32),
        compiler_params=pltpu.CompilerParams(dimension_semantics=("arbitrary",)),
        name="row_scatter",
    )(dest, fill_start, fill_n, _row_groups(x1e))
    return xs.reshape(p_rows, ROW_W)


def _moe_kernel(tg_ref, tv_ref, xs_ref, g2_ref, wg_ref, wu_ref, wd_ref, o_ref):
    i = pl.program_id(0)

    @pl.when(tv_ref[i] == 0)
    def _():
        o_ref[...] = jnp.zeros_like(o_ref)

    @pl.when(tv_ref[i] > 0)
    def _():
        x1 = xs_ref[:, 0:D_MODEL]
        cw = xs_ref[:, D_MODEL:D_MODEL + LANES]
        ms = jnp.mean(x1 * x1, axis=-1, keepdims=True)
        h2 = (x1 * lax.rsqrt(ms + EPS) * g2_ref[...]).astype(BF16)
        acc = x1
        for e in range(E_PER_GROUP):
            gate = _dot(h2, wg_ref[e])
            up = _dot(h2, wu_ref[e])
            act = (gate * jax.nn.sigmoid(gate)) * up * cw[:, e:e + 1]
            acc = acc + _dot(act.astype(BF16), wd_ref[e])
        o_ref[...] = acc


def _moe_call(tile_group, tile_valid, xs, g2, wg, wu, wd):
    p = xs.shape[0]
    t = TM_MOE
    by_block = lambda i, tg, tv: (i, 0)
    by_group = lambda i, tg, tv: (tg[i], 0, 0)
    once = pl.Buffered(1)
    grid_spec = pltpu.PrefetchScalarGridSpec(
        num_scalar_prefetch=2,
        grid=(p // t,),
        in_specs=[pl.BlockSpec((t, ROW_W), by_block),
                  pl.BlockSpec((1, D_MODEL), lambda i, *_: (0, 0)),
                  pl.BlockSpec((E_PER_GROUP, D_MODEL, D_EXPERT), by_group, pipeline_mode=once),
                  pl.BlockSpec((E_PER_GROUP, D_MODEL, D_EXPERT), by_group, pipeline_mode=once),
                  pl.BlockSpec((E_PER_GROUP, D_EXPERT, D_MODEL), by_group, pipeline_mode=once)],
        out_specs=pl.BlockSpec((t, D_MODEL), by_block),
    )
    return pl.pallas_call(
        _moe_kernel,
        grid_spec=grid_spec,
        out_shape=jax.ShapeDtypeStruct((p, D_MODEL), F32),
        compiler_params=pltpu.CompilerParams(dimension_semantics=("arbitrary",),
                                             vmem_limit_bytes=VMEM_LIMIT),
        name="moe_experts",
    )(tile_group, tile_valid, xs, g2, wg, wu, wd)


def _gather_kernel(dest_ref, ys_ref, o_ref, sem):
    _permute_block(o_ref, ys_ref, dest_ref, True, sem)


def _gather_call(dest, ys, n):
    assert n % TM_PERM == 0
    grid_spec = pltpu.PrefetchScalarGridSpec(
        num_scalar_prefetch=1,
        grid=(n // TM_PERM,),
        in_specs=[pl.BlockSpec(memory_space=pl.ANY)],
        out_specs=pl.BlockSpec((TM_PERM // SUBLANES, SUBLANES, D_MODEL), lambda i, *_: (i, 0, 0)),
        scratch_shapes=[pltpu.SemaphoreType.DMA(())],
    )
    out = pl.pallas_call(
        _gather_kernel,
        grid_spec=grid_spec,
        out_shape=jax.ShapeDtypeStruct((n // SUBLANES, SUBLANES, D_MODEL), F32),
        compiler_params=pltpu.CompilerParams(dimension_semantics=("arbitrary",)),
        name="row_gather",
    )(dest, _row_groups(ys))
    return out.reshape(n, D_MODEL)


def _router_rows(w_rg, b_rg, w_re, b_re):
    w = jnp.zeros((ROUTE_ROWS, D_MODEL), F32)
    w = w.at[0:N_GROUPS].set(w_rg.T).at[SUBLANES:SUBLANES + N_EXPERTS].set(w_re.T)
    b = jnp.zeros((ROUTE_ROWS, 1), F32)
    b = b.at[0:N_GROUPS, 0].set(b_rg).at[SUBLANES:SUBLANES + N_EXPERTS, 0].set(b_re)
    hi = w.astype(BF16)
    lo = (w - hi.astype(F32)).astype(BF16)
    return hi, lo, b


def _layer(x2d, batch, seq_len, norm_mix_g, w_in, q_norm_g, k_norm_g, conv_w, w_sb_branch,
           w_conv_branch, w_out, norm_ffn_g, w_router_group, b_router_group, w_router_expert,
           b_router_expert, w_gate_e, w_up_e, w_down_e):
    n = x2d.shape[0]
    lane = jnp.arange(WIDTH)
    head_mean = jnp.where(lane[:, None] // HEAD_DIM == lane[None, :] // HEAD_DIM,
                          1.0 / HEAD_DIM, 0.0).astype(BF16)
    qg = (jnp.tile(q_norm_g, N_HEADS) * HEAD_DIM ** -0.5)[None, :]
    kg = jnp.tile(k_norm_g, N_HEADS)[None, :]
    q, k, v, y, sa, sb = _inproj_call(x2d, norm_mix_g[None, :], w_in.astype(BF16), qg, kg,
                                      head_mean, conv_w, seq_len)

    pos = jnp.arange(TK)
    tri = (pos[:, None] >= pos[None, :]).astype(BF16)
    attn, (wg16, wu16, wd16) = _attn_call(q, k, v, tri, (w_gate_e, w_up_e, w_down_e), batch, seq_len)

    wrh, wrl, br = _router_rows(w_router_group, b_router_group, w_router_expert, b_router_expert)
    tpos = jnp.arange(TM_PROJ)
    upper = (tpos[:, None] < tpos[None, :]).astype(BF16)
    x1e, route, cnt = _merge_call(attn, y, sa, sb, x2d, w_sb_branch.astype(BF16),
                                  w_conv_branch.astype(BF16), w_out.astype(BF16),
                                  norm_ffn_g[None, :], wrh, wrl, br, upper)

    t = TM_MOE
    grp = route[0].astype(jnp.int32)
    rank = route[1].astype(jnp.int32)
    counts = cnt[0:N_GROUPS, 0].astype(jnp.int32)
    padded = ((counts + t - 1) // t) * t
    ends = jnp.cumsum(padded)
    offs = ends - padded
    dest = rank
    for g in range(N_GROUPS):
        dest = dest + jnp.where(grp == g, offs[g], 0)
    n_tiles = n // t + N_GROUPS
    tile_start = jnp.arange(n_tiles, dtype=jnp.int32) * t
    tile_valid = (tile_start < ends[-1]).astype(jnp.int32)
    tile_group = jnp.zeros_like(tile_start)
    for g in range(N_GROUPS - 1):
        tile_group = tile_group + (jnp.minimum(tile_start, ends[-1] - t) >= ends[g]).astype(jnp.int32)
    p_rows = n_tiles * t
    fill_start = jnp.concatenate([offs + counts, ends[-1:]])
    fill_n = jnp.concatenate([padded - counts, p_rows - ends[-1:]])

    xs = _scatter_call(dest, fill_start, fill_n, x1e, p_rows)
    ys = _moe_call(tile_group, tile_valid, xs, norm_ffn_g[None, :], wg16, wu16, wd16)
    return _gather_call(dest, ys, n)


def kernel(x, norm_mix_g, w_in, q_norm_g, k_norm_g, conv_w, w_sb_branch, w_conv_branch, w_out,
           norm_ffn_g, w_router_group, b_router_group, w_router_expert, b_router_expert,
           w_gate_e, w_up_e, w_down_e):
    batch, seq_len, d = x.shape
    x2d = x.reshape(batch * seq_len, d)
    for l in range(norm_mix_g.shape[0]):
        x2d = _layer(x2d, batch, seq_len, norm_mix_g[l], w_in[l], q_norm_g[l], k_norm_g[l],
                     conv_w[l], w_sb_branch[l], w_conv_branch[l], w_out[l], norm_ffn_g[l],
                     w_router_group[l], b_router_group[l], w_router_expert[l], b_router_expert[l],
                     w_gate_e[l], w_up_e[l], w_down_e[l])
    return x2d.reshape(batch, seq_len, d)
```

```python
import functools

import jax
import jax.numpy as jnp
from jax import lax
from jax.experimental import pallas as pl
from jax.experimental.pallas import tpu as pltpu

F32 = jnp.float32
BF16 = jnp.bfloat16

D_MODEL = 1024
N_HEADS = 8
HEAD_DIM = 64
WIDTH = N_HEADS * HEAD_DIM
CONV_K = 3
N_GROUPS = 4
E_PER_GROUP = 8
N_EXPERTS = N_GROUPS * E_PER_GROUP
D_EXPERT = 256
EPS = 1e-6

LANES = 128
SUBLANES = 8
HEADS_PER_STEP = LANES // HEAD_DIM
N_PAIRS = N_HEADS // HEADS_PER_STEP

TM_PROJ = 512
TQ = 256
TK = 256
TM_MOE = 512
TM_PERM = 512
ROW_W = D_MODEL + LANES
ISSUE_UNROLL = 8
ROUTE_ROWS = 48
VMEM_LIMIT = 56 * 1024 * 1024

_NT = (((1,), (1,)), ((), ()))


def _dot(a, b):
    return jnp.dot(a, b, preferred_element_type=F32)


def _split_bf16(v):
    hi = v.astype(BF16)
    lo = (v - hi.astype(F32)).astype(BF16)
    return hi, lo


def _inproj_kernel(x_ref, g1_ref, w_ref, qg_ref, kg_ref, pm_ref, cw_ref,
                   qt_ref, k_ref, vt_ref, y_ref, sa_ref, sb_ref, cu_scr, *, tiles_per_seq):
    tm = x_ref.shape[0]
    x = x_ref[...]
    ms = jnp.mean(x * x, axis=-1, keepdims=True)
    h = (x * lax.rsqrt(ms + EPS) * g1_ref[...]).astype(BF16)

    def proj(lo, width):
        return _dot(h, w_ref[:, lo:lo + width])

    def head_norm(t, g_ref):
        hms = _dot((t * t).astype(BF16), pm_ref[...])
        return t * lax.rsqrt(hms + EPS) * g_ref[...]

    q = head_norm(proj(0, WIDTH), qg_ref)
    v = proj(2 * WIDTH, WIDTH)
    for blk in range(tm // TQ):
        qt_ref[blk] = q[blk * TQ:(blk + 1) * TQ, :].T.astype(BF16)
        vt_ref[blk] = v[blk * TK:(blk + 1) * TK, :].T.astype(BF16)
    k_ref[...] = head_norm(proj(WIDTH, WIDTH), kg_ref).astype(BF16)

    c_b = proj(3 * WIDTH, WIDTH)
    cu = proj(4 * WIDTH, WIDTH) * proj(5 * WIDTH, WIDTH)

    @pl.when(pl.program_id(0) % tiles_per_seq == 0)
    def _():
        cu_scr[0:SUBLANES, :] = jnp.zeros((SUBLANES, WIDTH), F32)

    cu_scr[SUBLANES:SUBLANES + tm, :] = cu
    cw = cw_ref[...]
    conv = (cw[0:1, :] * cu_scr[SUBLANES - 2:SUBLANES - 2 + tm, :]
            + cw[1:2, :] * cu_scr[SUBLANES - 1:SUBLANES - 1 + tm, :]
            + cw[2:3, :] * cu)
    y_ref[...] = (c_b * conv).astype(BF16)
    cu_scr[0:SUBLANES, :] = cu_scr[tm:tm + SUBLANES, :]

    sa_ref[...] = jax.nn.sigmoid(proj(6 * WIDTH, D_MODEL))
    sb_ref[...] = jax.nn.sigmoid(proj(6 * WIDTH + D_MODEL, D_MODEL))


def _inproj_call(x2d, g1, w_in, qg, kg, pm, conv_w, seq_len):
    n = x2d.shape[0]
    tm = TM_PROJ
    in_w = w_in.shape[1]
    const = lambda shape: pl.BlockSpec(shape, lambda i: (0, 0), pipeline_mode=pl.Buffered(1))
    rows = lambda width: pl.BlockSpec((tm, width), lambda i: (i, 0))
    assert TQ == TK and tm % TQ == 0
    cols = pl.BlockSpec((tm // TQ, WIDTH, TQ), lambda i: (i, 0, 0))
    cols_shape = jax.ShapeDtypeStruct((n // TQ, WIDTH, TQ), BF16)
    return pl.pallas_call(
        functools.partial(_inproj_kernel, tiles_per_seq=seq_len // tm),
        grid=(n // tm,),
        in_specs=[rows(D_MODEL), const((1, D_MODEL)), const((D_MODEL, in_w)), const((1, WIDTH)),
                  const((1, WIDTH)), const((WIDTH, WIDTH)), const((CONV_K, WIDTH))],
        out_specs=[cols, rows(WIDTH), cols, rows(WIDTH), rows(D_MODEL), rows(D_MODEL)],
        out_shape=[cols_shape, jax.ShapeDtypeStruct((n, WIDTH), BF16)] * 2
                  + [jax.ShapeDtypeStruct((n, D_MODEL), F32)] * 2,
        scratch_shapes=[pltpu.VMEM((SUBLANES + tm, WIDTH), F32)],
        compiler_params=pltpu.CompilerParams(dimension_semantics=("arbitrary",),
                                             vmem_limit_bytes=VMEM_LIMIT),
        name="inproj",
    )(x2d, g1, w_in, qg, kg, pm, conv_w)


def _softplus(z):
    return jnp.maximum(z, 0.0) + jnp.log(1.0 + jnp.exp(-jnp.abs(z)))


def _attn_kernel(qt_ref, k_ref, vt_ref, tri_ref, wg_ref, wu_ref, wd_ref, o_ref, wg16_ref, wu16_ref,
                 wd16_ref, qm_scr, z_scr, zc_scr, suf_scr, a_scr, acc_scr, c_scr):
    i = pl.program_id(1)
    tq = qt_ref.shape[2]
    feat = lax.broadcasted_iota(jnp.int32, (LANES, tq), 0)
    first_head = feat < HEAD_DIM
    for p in range(N_PAIRS):
        q2 = qt_ref[0, p * LANES:(p + 1) * LANES, :]
        zero = jnp.zeros_like(q2)
        qm_scr[2 * p] = jnp.where(first_head, q2, zero)
        qm_scr[2 * p + 1] = jnp.where(first_head, zero, q2)
    acc_scr[...] = jnp.zeros_like(acc_scr)
    c_scr[...] = jnp.zeros_like(c_scr)

    def pair(s):
        p = s // HEADS_PER_STEP
        return slice(p * LANES, (p + 1) * LANES)

    def scores(s, j):
        keys = k_ref[pl.ds(pl.multiple_of(j * TK, TK), TK), pair(s)]
        z_scr[s] = _dot(keys, qm_scr[s])

    def suffix_sums(s, mask):
        z = z_scr[s]
        zc_scr[s] = z - c_scr[s]
        sp = _softplus(z.astype(BF16))
        if mask is not None:
            sp = jnp.where(mask, sp, jnp.zeros_like(sp))
        suf_scr[s] = _dot(tri_ref[...], sp)

    def weights(s, mask):
        suffix = suf_scr[s]
        a = jnp.exp(zc_scr[s] - suffix)
        if mask is not None:
            a = jnp.where(mask, a, 0.0)
        a_scr[s] = a.astype(BF16)
        c_scr[s] = c_scr[s] + suffix[0:1, :]

    def accumulate(s, j):
        acc_scr[s] = acc_scr[s] + _dot(vt_ref[j, pair(s), :], a_scr[s])

    key = lax.broadcasted_iota(jnp.int32, (TK, tq), 0)
    qry = lax.broadcasted_iota(jnp.int32, (TK, tq), 1)
    causal = key < qry
    nxt = jnp.maximum(i - 1, 0)
    for s in range(N_HEADS):
        scores(s, i)
    for s in range(N_HEADS):
        suffix_sums(s, causal)
        scores(s, nxt)
    for s in range(N_HEADS):
        weights(s, causal)

    def body(t, carry):
        j = i - t
        nxt = jnp.maximum(j - 1, 0)
        for s in range(N_HEADS):
            accumulate(s, j + 1)
            suffix_sums(s, None)
            scores(s, nxt)
        for s in range(N_HEADS):
            weights(s, None)
        return carry

    lax.fori_loop(1, i + 1, body, 0)
    for s in range(N_HEADS):
        accumulate(s, 0)
    for p in range(N_PAIRS):
        both = jnp.where(first_head, acc_scr[2 * p], acc_scr[2 * p + 1])
        o_ref[:, p * LANES:(p + 1) * LANES] = both.T.astype(o_ref.dtype)

    wg16_ref[...] = wg_ref[...].astype(BF16)
    wu16_ref[...] = wu_ref[...].astype(BF16)
    wd16_ref[...] = wd_ref[...].astype(BF16)


def _attn_call(qt, k, vt, tri, expert_weights, batch, seq_len):
    assert TQ == TK
    nq = seq_len // TQ
    steps = batch * nq
    qspec = pl.BlockSpec((1, WIDTH, TQ), lambda b, i: (b * nq + i, 0, 0))
    kspec = pl.BlockSpec((seq_len, WIDTH), lambda b, i: (b, 0))
    vspec = pl.BlockSpec((seq_len // TK, WIDTH, TK), lambda b, i: (b, 0, 0))
    ospec = pl.BlockSpec((TQ, WIDTH), lambda b, i: (b * nq + i, 0))
    slabs = [w.reshape(steps, -1, w.shape[-1]) for w in expert_weights]
    wspecs = [pl.BlockSpec((1,) + s.shape[1:], lambda b, i: (b * nq + i, 0, 0)) for s in slabs]
    attn, *cast = pl.pallas_call(
        _attn_kernel,
        grid=(batch, nq),
        in_specs=[qspec, kspec, vspec, pl.BlockSpec((TK, TK), lambda b, i: (0, 0))] + wspecs,
        out_specs=[ospec] + wspecs,
        out_shape=[jax.ShapeDtypeStruct(k.shape, BF16)]
                  + [jax.ShapeDtypeStruct(s.shape, BF16) for s in slabs],
        scratch_shapes=[pltpu.VMEM((N_HEADS, LANES, TQ), BF16),
                        pltpu.VMEM((N_HEADS, TK, TQ), F32),
                        pltpu.VMEM((N_HEADS, TK, TQ), F32),
                        pltpu.VMEM((N_HEADS, TK, TQ), F32),
                        pltpu.VMEM((N_HEADS, TK, TQ), BF16),
                        pltpu.VMEM((N_HEADS, LANES, TQ), F32),
                        pltpu.VMEM((N_HEADS, 1, TQ), F32)],
        compiler_params=pltpu.CompilerParams(dimension_semantics=("arbitrary", "arbitrary"),
                                             vmem_limit_bytes=VMEM_LIMIT),
        name="sb_attn",
    )(qt, k, vt, tri, *slabs)
    return attn, [c.reshape(w.shape) for c, w in zip(cast, expert_weights)]


def _merge_kernel(attn_ref, y_ref, sa_ref, sb_ref, x_ref, wsb_ref, wcv_ref, wout_ref, g2_ref,
                  wrh_ref, wrl_ref, br_ref, upper_ref, x1e_ref, route_ref, cnt_ref, cnt_scr):
    tm = x_ref.shape[0]
    branch_a = _dot(attn_ref[...], wsb_ref[...])
    branch_b = _dot(y_ref[...], wcv_ref[...])
    merged = (sa_ref[...] * branch_a + sb_ref[...] * branch_b).astype(BF16)
    x1 = x_ref[...] + _dot(merged, wout_ref[...])
    x1e_ref[:, 0:D_MODEL] = x1
    ms = jnp.mean(x1 * x1, axis=-1, keepdims=True)
    h2 = x1 * lax.rsqrt(ms + EPS) * g2_ref[...]

    hh, hl = _split_bf16(h2)
    nt = functools.partial(lax.dot_general, dimension_numbers=_NT, preferred_element_type=F32)
    logits = nt(wrh_ref[...], hh) + nt(wrh_ref[...], hl) + nt(wrl_ref[...], hh) + br_ref[...]

    rowi = lax.broadcasted_iota(jnp.int32, (SUBLANES, tm), 0).astype(F32)
    neg = jnp.float32(-jnp.inf)
    big = jnp.float32(SUBLANES)

    def first_argmax(vals):
        top = jnp.max(vals, axis=0, keepdims=True)
        idx = jnp.min(jnp.where(vals == top, rowi, big), axis=0, keepdims=True)
        return top, idx

    gl = jnp.where(rowi < N_GROUPS, logits[0:SUBLANES, :], neg)
    gmax, gidx = first_argmax(gl)
    g_w = 1.0 / jnp.sum(jnp.exp(gl - gmax), axis=0, keepdims=True)
    el = logits[SUBLANES:2 * SUBLANES, :]
    for g in range(1, N_GROUPS):
        el = jnp.where(gidx == g, logits[(g + 1) * SUBLANES:(g + 2) * SUBLANES, :], el)
    t1, i1 = first_argmax(el)
    t2, i2 = first_argmax(jnp.where(rowi == i1, neg, el))
    e = jnp.exp(t2 - t1)
    w1 = g_w / (1.0 + e)
    w2 = g_w * e / (1.0 + e)

    cw_t = jnp.where(rowi == i1, w1, 0.0) + jnp.where(rowi == i2, w2, 0.0)
    cw_t = jnp.concatenate([cw_t, jnp.zeros((LANES - SUBLANES, tm), F32)], axis=0)
    x1e_ref[:, D_MODEL:D_MODEL + LANES] = cw_t.T

    @pl.when(pl.program_id(0) == 0)
    def _():
        cnt_scr[...] = jnp.zeros_like(cnt_scr)

    onehot = jnp.where(rowi == gidx, 1.0, 0.0)
    before = _dot(onehot.astype(BF16), upper_ref[...]) + cnt_scr[:, 0:1]
    rank = jnp.sum(onehot * before, axis=0, keepdims=True)
    cnt_scr[...] = cnt_scr[...] + jnp.sum(onehot, axis=1, keepdims=True)
    cnt_ref[...] = cnt_scr[...]
    route_ref[...] = jnp.where(rowi == 0, gidx, jnp.where(rowi == 1, rank, 0.0))


def _merge_call(attn, y, sa, sb, x2d, wsb, wcv, wout, g2, wrh, wrl, br, upper):
    n = x2d.shape[0]
    tm = TM_PROJ
    rows = lambda width: pl.BlockSpec((tm, width), lambda i: (i, 0))
    const = lambda shape: pl.BlockSpec(shape, lambda i: (0, 0), pipeline_mode=pl.Buffered(1))
    return pl.pallas_call(
        _merge_kernel,
        grid=(n // tm,),
        in_specs=[rows(WIDTH), rows(WIDTH), rows(D_MODEL), rows(D_MODEL), rows(D_MODEL),
                  const((WIDTH, D_MODEL)), const((WIDTH, D_MODEL)), const((D_MODEL, D_MODEL)),
                  const((1, D_MODEL)), const((ROUTE_ROWS, D_MODEL)), const((ROUTE_ROWS, D_MODEL)),
                  const((ROUTE_ROWS, 1)), const((tm, tm))],
        out_specs=[rows(ROW_W), pl.BlockSpec((SUBLANES, tm), lambda i: (0, i)),
                   pl.BlockSpec((SUBLANES, LANES), lambda i: (0, 0))],
        out_shape=[jax.ShapeDtypeStruct((n, ROW_W), F32), jax.ShapeDtypeStruct((SUBLANES, n), F32),
                   jax.ShapeDtypeStruct((SUBLANES, LANES), F32)],
        scratch_shapes=[pltpu.VMEM((SUBLANES, LANES), F32)],
        compiler_params=pltpu.CompilerParams(dimension_semantics=("arbitrary",),
                                             vmem_limit_bytes=VMEM_LIMIT),
        name="merge_router",
    )(attn, y, sa, sb, x2d, wsb, wcv, wout, g2, wrh, wrl, br, upper)


def _row_copy(src_ref, src_row, dst_ref, dst_row, sem):
    return pltpu.make_async_copy(src_ref.at[pl.ds(src_row, 1)], dst_ref.at[pl.ds(dst_row, 1)], sem)


def _scatter_kernel(dest_ref, fill_start_ref, fill_n_ref, x_ref, xs_ref, zero_scr, sem):
    i = pl.program_id(0)
    tm = x_ref.shape[0]
    base = i * tm

    def start(r, carry):
        _row_copy(x_ref, r, xs_ref, dest_ref[base + r], sem).start()
        return carry

    lax.fori_loop(0, tm, start, 0, unroll=ISSUE_UNROLL)
    pltpu.make_async_copy(x_ref, xs_ref.at[pl.ds(0, tm)], sem).wait()

    @pl.when(i == pl.num_programs(0) - 1)
    def _():
        zero_scr[...] = jnp.zeros_like(zero_scr)
        for g in range(N_GROUPS + 1):
            first = fill_start_ref[g]

            def start_fill(r, carry):
                _row_copy(zero_scr, 0, xs_ref, first + r, sem).start()
                return carry

            def wait_fill(r, carry):
                _row_copy(zero_scr, 0, xs_ref, 0, sem).wait()
                return carry

            lax.fori_loop(0, fill_n_ref[g], start_fill, 0)
            lax.fori_loop(0, fill_n_ref[g], wait_fill, 0)


def _scatter_call(dest, fill_start, fill_n, x1e, p_rows):
    n = x1e.shape[0]
    tm = TM_PERM
    grid_spec = pltpu.PrefetchScalarGridSpec(
        num_scalar_prefetch=3,
        grid=(n // tm,),
        in_specs=[pl.BlockSpec((tm, ROW_W), lambda i, *_: (i, 0))],
        out_specs=pl.BlockSpec(memory_space=pl.ANY),
        scratch_shapes=[pltpu.VMEM((SUBLANES, ROW_W), F32), pltpu.SemaphoreType.DMA(())],
    )
    return pl.pallas_call(
        _scatter_kernel,
        grid_spec=grid_spec,
        out_shape=jax.ShapeDtypeStruct((p_rows, ROW_W), F32),
        compiler_params=pltpu.CompilerParams(dimension_semantics=("arbitrary",),
                                             vmem_limit_bytes=VMEM_LIMIT),
        name="row_scatter",
    )(dest, fill_start, fill_n, x1e)


def _moe_kernel(tg_ref, tv_ref, xs_ref, g2_ref, wg_ref, wu_ref, wd_ref, o_ref):
    i = pl.program_id(0)

    @pl.when(tv_ref[i] == 0)
    def _():
        o_ref[...] = jnp.zeros_like(o_ref)

    @pl.when(tv_ref[i] > 0)
    def _():
        x1 = xs_ref[:, 0:D_MODEL]
        cw = xs_ref[:, D_MODEL:D_MODEL + LANES]
        ms = jnp.mean(x1 * x1, axis=-1, keepdims=True)
        h2 = (x1 * lax.rsqrt(ms + EPS) * g2_ref[...]).astype(BF16)
        acc = x1
        for e in range(E_PER_GROUP):
            gate = _dot(h2, wg_ref[e])
            up = _dot(h2, wu_ref[e])
            act = (gate * jax.nn.sigmoid(gate)) * up * cw[:, e:e + 1]
            acc = acc + _dot(act.astype(BF16), wd_ref[e])
        o_ref[...] = acc


def _moe_call(tile_group, tile_valid, xs, g2, wg, wu, wd):
    p = xs.shape[0]
    t = TM_MOE
    by_block = lambda i, tg, tv: (i, 0)
    by_group = lambda i, tg, tv: (tg[i], 0, 0)
    once = pl.Buffered(1)
    grid_spec = pltpu.PrefetchScalarGridSpec(
        num_scalar_prefetch=2,
        grid=(p // t,),
        in_specs=[pl.BlockSpec((t, ROW_W), by_block),
                  pl.BlockSpec((1, D_MODEL), lambda i, *_: (0, 0)),
                  pl.BlockSpec((E_PER_GROUP, D_MODEL, D_EXPERT), by_group, pipeline_mode=once),
                  pl.BlockSpec((E_PER_GROUP, D_MODEL, D_EXPERT), by_group, pipeline_mode=once),
                  pl.BlockSpec((E_PER_GROUP, D_EXPERT, D_MODEL), by_group, pipeline_mode=once)],
        out_specs=pl.BlockSpec((t, D_MODEL), by_block),
    )
    return pl.pallas_call(
        _moe_kernel,
        grid_spec=grid_spec,
        out_shape=jax.ShapeDtypeStruct((p, D_MODEL), F32),
        compiler_params=pltpu.CompilerParams(dimension_semantics=("arbitrary",),
                                             vmem_limit_bytes=VMEM_LIMIT),
        name="moe_experts",
    )(tile_group, tile_valid, xs, g2, wg, wu, wd)


def _gather_kernel(dest_ref, ys_ref, o_ref, sem):
    i = pl.program_id(0)
    tm = o_ref.shape[0]
    base = i * tm

    def start(r, carry):
        _row_copy(ys_ref, dest_ref[base + r], o_ref, r, sem).start()
        return carry

    lax.fori_loop(0, tm, start, 0, unroll=ISSUE_UNROLL)
    pltpu.make_async_copy(ys_ref.at[pl.ds(0, tm)], o_ref, sem).wait()


def _gather_call(dest, ys, n):
    tm = TM_PERM
    grid_spec = pltpu.PrefetchScalarGridSpec(
        num_scalar_prefetch=1,
        grid=(n // tm,),
        in_specs=[pl.BlockSpec(memory_space=pl.ANY)],
        out_specs=pl.BlockSpec((tm, D_MODEL), lambda i, *_: (i, 0)),
        scratch_shapes=[pltpu.SemaphoreType.DMA(())],
    )
    return pl.pallas_call(
        _gather_kernel,
        grid_spec=grid_spec,
        out_shape=jax.ShapeDtypeStruct((n, D_MODEL), F32),
        compiler_params=pltpu.CompilerParams(dimension_semantics=("arbitrary",),
                                             vmem_limit_bytes=VMEM_LIMIT),
        name="row_gather",
    )(dest, ys)


def _router_rows(w_rg, b_rg, w_re, b_re):
    w = jnp.zeros((ROUTE_ROWS, D_MODEL), F32)
    w = w.at[0:N_GROUPS].set(w_rg.T).at[SUBLANES:SUBLANES + N_EXPERTS].set(w_re.T)
    b = jnp.zeros((ROUTE_ROWS, 1), F32)
    b = b.at[0:N_GROUPS, 0].set(b_rg).at[SUBLANES:SUBLANES + N_EXPERTS, 0].set(b_re)
    hi = w.astype(BF16)
    lo = (w - hi.astype(F32)).astype(BF16)
    return hi, lo, b


def _layer(x2d, batch, seq_len, norm_mix_g, w_in, q_norm_g, k_norm_g, conv_w, w_sb_branch,
           w_conv_branch, w_out, norm_ffn_g, w_router_group, b_router_group, w_router_expert,
           b_router_expert, w_gate_e, w_up_e, w_down_e):
    n = x2d.shape[0]
    lane = jnp.arange(WIDTH)
    head_mean = jnp.where(lane[:, None] // HEAD_DIM == lane[None, :] // HEAD_DIM,
                          1.0 / HEAD_DIM, 0.0).astype(BF16)
    qg = (jnp.tile(q_norm_g, N_HEADS) * HEAD_DIM ** -0.5)[None, :]
    kg = jnp.tile(k_norm_g, N_HEADS)[None, :]
    qt, k, vt, y, sa, sb = _inproj_call(x2d, norm_mix_g[None, :], w_in.astype(BF16), qg, kg,
                                        head_mean, conv_w, seq_len)

    pos = jnp.arange(TK)
    tri = (pos[None, :] >= pos[:, None]).astype(BF16)
    attn, (wg16, wu16, wd16) = _attn_call(qt, k, vt, tri, (w_gate_e, w_up_e, w_down_e), batch,
                                          seq_len)

    wrh, wrl, br = _router_rows(w_router_group, b_router_group, w_router_expert, b_router_expert)
    tpos = jnp.arange(TM_PROJ)
    upper = (tpos[:, None] < tpos[None, :]).astype(BF16)
    x1e, route, cnt = _merge_call(attn, y, sa, sb, x2d, w_sb_branch.astype(BF16),
                                  w_conv_branch.astype(BF16), w_out.astype(BF16),
                                  norm_ffn_g[None, :], wrh, wrl, br, upper)

    t = TM_MOE
    grp = route[0].astype(jnp.int32)
    rank = route[1].astype(jnp.int32)
    counts = cnt[0:N_GROUPS, 0].astype(jnp.int32)
    padded = ((counts + t - 1) // t) * t
    ends = jnp.cumsum(padded)
    offs = ends - padded
    dest = rank
    for g in range(N_GROUPS):
        dest = dest + jnp.where(grp == g, offs[g], 0)
    n_tiles = n // t + N_GROUPS
    tile_start = jnp.arange(n_tiles, dtype=jnp.int32) * t
    tile_valid = (tile_start < ends[-1]).astype(jnp.int32)
    tile_group = jnp.zeros_like(tile_start)
    for g in range(N_GROUPS - 1):
        tile_group = tile_group + (jnp.minimum(tile_start, ends[-1] - t) >= ends[g]).astype(jnp.int32)
    p_rows = n_tiles * t
    fill_start = jnp.concatenate([offs + counts, ends[-1:]])
    fill_n = jnp.concatenate([padded - counts, p_rows - ends[-1:]])

    xs = _scatter_call(dest, fill_start, fill_n, x1e, p_rows)
    ys = _moe_call(tile_group, tile_valid, xs, norm_ffn_g[None, :], wg16, wu16, wd16)
    return _gather_call(dest, ys, n)


def kernel(x, norm_mix_g, w_in, q_norm_g, k_norm_g, conv_w, w_sb_branch, w_conv_branch, w_out,
           norm_ffn_g, w_router_group, b_router_group, w_router_expert, b_router_expert,
           w_gate_e, w_up_e, w_down_e):
    batch, seq_len, d = x.shape
    x2d = x.reshape(batch * seq_len, d)
    for l in range(norm_mix_g.shape[0]):
        x2d = _layer(x2d, batch, seq_len, norm_mix_g[l], w_in[l], q_norm_g[l], k_norm_g[l],
                     conv_w[l], w_sb_branch[l], w_conv_branch[l], w_out[l], norm_ffn_g[l],
                     w_router_group[l], b_router_group[l], w_router_expert[l], b_router_expert[l],
                     w_gate_e[l], w_up_e[l], w_down_e[l])
    return x2d.reshape(batch, seq_len, d)
```

```python
import functools

import jax
import jax.numpy as jnp
from jax import lax
from jax.experimental import pallas as pl
from jax.experimental.pallas import tpu as pltpu

F32 = jnp.float32
BF16 = jnp.bfloat16

D_MODEL = 1024
N_HEADS = 8
HEAD_DIM = 64
WIDTH = N_HEADS * HEAD_DIM
CONV_K = 3
N_GROUPS = 4
E_PER_GROUP = 8
N_EXPERTS = N_GROUPS * E_PER_GROUP
D_EXPERT = 256
EPS = 1e-6

LANES = 128
SUBLANES = 8
HEADS_PER_STEP = LANES // HEAD_DIM
N_PAIRS = N_HEADS // HEADS_PER_STEP

TM_PROJ = 512
TQ = 256
TK = 256
TM_MOE = 512
TM_PERM = 512
ISSUE_UNROLL = 8
VMEM_LIMIT = 56 * 1024 * 1024

def _dot(a, b):
    return jnp.dot(a, b, preferred_element_type=F32)


def _split_bf16(v):
    hi = v.astype(BF16)
    lo = (v - hi.astype(F32)).astype(BF16)
    return hi, lo


def _inproj_kernel(x_ref, g1_ref, w_ref, qg_ref, kg_ref, pm_ref, cw_ref,
                   qt_ref, k_ref, vt_ref, y_ref, sa_ref, sb_ref, cu_scr, *, tiles_per_seq):
    tm = x_ref.shape[0]
    x = x_ref[...]
    ms = jnp.mean(x * x, axis=-1, keepdims=True)
    h = (x * lax.rsqrt(ms + EPS) * g1_ref[...]).astype(BF16)

    def proj(lo, width):
        return _dot(h, w_ref[:, lo:lo + width])

    def head_norm(t, g_ref):
        hms = _dot((t * t).astype(BF16), pm_ref[...])
        return t * lax.rsqrt(hms + EPS) * g_ref[...]

    q = head_norm(proj(0, WIDTH), qg_ref)
    v = proj(2 * WIDTH, WIDTH)
    for blk in range(tm // TQ):
        qt_ref[blk] = q[blk * TQ:(blk + 1) * TQ, :].T.astype(BF16)
        vt_ref[blk] = v[blk * TK:(blk + 1) * TK, :].T.astype(BF16)
    k_ref[...] = head_norm(proj(WIDTH, WIDTH), kg_ref).astype(BF16)

    c_b = proj(3 * WIDTH, WIDTH)
    cu = proj(4 * WIDTH, WIDTH) * proj(5 * WIDTH, WIDTH)

    @pl.when(pl.program_id(0) % tiles_per_seq == 0)
    def _():
        cu_scr[0:SUBLANES, :] = jnp.zeros((SUBLANES, WIDTH), F32)

    cu_scr[SUBLANES:SUBLANES + tm, :] = cu
    cw = cw_ref[...]
    conv = (cw[0:1, :] * cu_scr[SUBLANES - 2:SUBLANES - 2 + tm, :]
            + cw[1:2, :] * cu_scr[SUBLANES - 1:SUBLANES - 1 + tm, :]
            + cw[2:3, :] * cu)
    y_ref[...] = (c_b * conv).astype(BF16)
    cu_scr[0:SUBLANES, :] = cu_scr[tm:tm + SUBLANES, :]

    sa_ref[...] = jax.nn.sigmoid(proj(6 * WIDTH, D_MODEL))
    sb_ref[...] = jax.nn.sigmoid(proj(6 * WIDTH + D_MODEL, D_MODEL))


def _inproj_call(x2d, g1, w_in, qg, kg, pm, conv_w, seq_len):
    n = x2d.shape[0]
    tm = TM_PROJ
    in_w = w_in.shape[1]
    const = lambda shape: pl.BlockSpec(shape, lambda i: (0, 0), pipeline_mode=pl.Buffered(1))
    rows = lambda width: pl.BlockSpec((tm, width), lambda i: (i, 0))
    assert TQ == TK and tm % TQ == 0
    cols = pl.BlockSpec((tm // TQ, WIDTH, TQ), lambda i: (i, 0, 0))
    cols_shape = jax.ShapeDtypeStruct((n // TQ, WIDTH, TQ), BF16)
    return pl.pallas_call(
        functools.partial(_inproj_kernel, tiles_per_seq=seq_len // tm),
        grid=(n // tm,),
        in_specs=[rows(D_MODEL), const((1, D_MODEL)), const((D_MODEL, in_w)), const((1, WIDTH)),
                  const((1, WIDTH)), const((WIDTH, WIDTH)), const((CONV_K, WIDTH))],
        out_specs=[cols, rows(WIDTH), cols, rows(WIDTH), rows(D_MODEL), rows(D_MODEL)],
        out_shape=[cols_shape, jax.ShapeDtypeStruct((n, WIDTH), BF16)] * 2
                  + [jax.ShapeDtypeStruct((n, D_MODEL), F32)] * 2,
        scratch_shapes=[pltpu.VMEM((SUBLANES + tm, WIDTH), F32)],
        compiler_params=pltpu.CompilerParams(dimension_semantics=("arbitrary",),
                                             vmem_limit_bytes=VMEM_LIMIT),
        name="inproj",
    )(x2d, g1, w_in, qg, kg, pm, conv_w)


def _softplus(z):
    return jnp.maximum(z, 0.0) + jnp.log(1.0 + jnp.exp(-jnp.abs(z)))


def _attn_kernel(qt_ref, k_ref, vt_ref, tri_ref, wg_ref, wu_ref, wd_ref, o_ref, wg16_ref, wu16_ref,
                 wd16_ref, qm_scr, z_scr, zc_scr, suf_scr, a_scr, acc_scr, c_scr):
    i = pl.program_id(1)
    tq = qt_ref.shape[2]
    feat = lax.broadcasted_iota(jnp.int32, (LANES, tq), 0)
    first_head = feat < HEAD_DIM
    for p in range(N_PAIRS):
        q2 = qt_ref[0, p * LANES:(p + 1) * LANES, :]
        zero = jnp.zeros_like(q2)
        qm_scr[2 * p] = jnp.where(first_head, q2, zero)
        qm_scr[2 * p + 1] = jnp.where(first_head, zero, q2)
    acc_scr[...] = jnp.zeros_like(acc_scr)
    c_scr[...] = jnp.zeros_like(c_scr)

    def pair(s):
        p = s // HEADS_PER_STEP
        return slice(p * LANES, (p + 1) * LANES)

    def scores(s, j):
        keys = k_ref[pl.ds(pl.multiple_of(j * TK, TK), TK), pair(s)]
        z_scr[s] = _dot(keys, qm_scr[s])

    def suffix_sums(s, mask):
        z = z_scr[s]
        zc_scr[s] = z - c_scr[s]
        sp = _softplus(z.astype(BF16))
        if mask is not None:
            sp = jnp.where(mask, sp, jnp.zeros_like(sp))
        suf_scr[s] = _dot(tri_ref[...], sp)

    def weights(s, mask):
        suffix = suf_scr[s]
        a = jnp.exp(zc_scr[s] - suffix)
        if mask is not None:
            a = jnp.where(mask, a, 0.0)
        a_scr[s] = a.astype(BF16)
        c_scr[s] = c_scr[s] + suffix[0:1, :]

    def accumulate(s, j):
        acc_scr[s] = acc_scr[s] + _dot(vt_ref[j, pair(s), :], a_scr[s])

    key = lax.broadcasted_iota(jnp.int32, (TK, tq), 0)
    qry = lax.broadcasted_iota(jnp.int32, (TK, tq), 1)
    causal = key < qry
    nxt = jnp.maximum(i - 1, 0)
    for s in range(N_HEADS):
        scores(s, i)
    for s in range(N_HEADS):
        suffix_sums(s, causal)
        scores(s, nxt)
    for s in range(N_HEADS):
        weights(s, causal)

    def body(t, carry):
        j = i - t
        nxt = jnp.maximum(j - 1, 0)
        for s in range(N_HEADS):
            accumulate(s, j + 1)
            suffix_sums(s, None)
            scores(s, nxt)
        for s in range(N_HEADS):
            weights(s, None)
        return carry

    lax.fori_loop(1, i + 1, body, 0)
    for s in range(N_HEADS):
        accumulate(s, 0)
    for p in range(N_PAIRS):
        both = jnp.where(first_head, acc_scr[2 * p], acc_scr[2 * p + 1])
        o_ref[:, p * LANES:(p + 1) * LANES] = both.T.astype(o_ref.dtype)

    wg16_ref[...] = wg_ref[...].astype(BF16)
    wu16_ref[...] = wu_ref[...].astype(BF16)
    wd16_ref[...] = wd_ref[...].astype(BF16)


def _attn_call(qt, k, vt, tri, expert_weights, batch, seq_len):
    assert TQ == TK
    nq = seq_len // TQ
    steps = batch * nq
    qspec = pl.BlockSpec((1, WIDTH, TQ), lambda b, i: (b * nq + i, 0, 0))
    kspec = pl.BlockSpec((seq_len, WIDTH), lambda b, i: (b, 0))
    vspec = pl.BlockSpec((seq_len // TK, WIDTH, TK), lambda b, i: (b, 0, 0))
    ospec = pl.BlockSpec((TQ, WIDTH), lambda b, i: (b * nq + i, 0))
    slabs = [w.reshape(steps, -1, w.shape[-1]) for w in expert_weights]
    wspecs = [pl.BlockSpec((1,) + s.shape[1:], lambda b, i: (b * nq + i, 0, 0)) for s in slabs]
    attn, *cast = pl.pallas_call(
        _attn_kernel,
        grid=(batch, nq),
        in_specs=[qspec, kspec, vspec, pl.BlockSpec((TK, TK), lambda b, i: (0, 0))] + wspecs,
        out_specs=[ospec] + wspecs,
        out_shape=[jax.ShapeDtypeStruct(k.shape, BF16)]
                  + [jax.ShapeDtypeStruct(s.shape, BF16) for s in slabs],
        scratch_shapes=[pltpu.VMEM((N_HEADS, LANES, TQ), BF16),
                        pltpu.VMEM((N_HEADS, TK, TQ), F32),
                        pltpu.VMEM((N_HEADS, TK, TQ), F32),
                        pltpu.VMEM((N_HEADS, TK, TQ), F32),
                        pltpu.VMEM((N_HEADS, TK, TQ), BF16),
                        pltpu.VMEM((N_HEADS, LANES, TQ), F32),
                        pltpu.VMEM((N_HEADS, 1, TQ), F32)],
        compiler_params=pltpu.CompilerParams(dimension_semantics=("arbitrary", "arbitrary"),
                                             vmem_limit_bytes=VMEM_LIMIT),
        name="sb_attn",
    )(qt, k, vt, tri, *slabs)
    return attn, [c.reshape(w.shape) for c, w in zip(cast, expert_weights)]


def _router_logits(w_hi_ref, w_lo_ref, b_ref, h):
    hh, hl = _split_bf16(h)
    logits = _dot(hh, w_hi_ref[0]) + _dot(hl, w_hi_ref[0]) + _dot(hh, w_lo_ref[0]) + b_ref[0]
    return logits.T


def _first_argmax(vals, rowi):
    top = jnp.max(vals, axis=0, keepdims=True)
    idx = jnp.min(jnp.where(vals == top, rowi, jnp.float32(SUBLANES)), axis=0, keepdims=True)
    return top, idx


def _ffn_norm(x1, g2_ref):
    ms = jnp.mean(x1 * x1, axis=-1, keepdims=True)
    return x1 * lax.rsqrt(ms + EPS) * g2_ref[...]


def _merge_kernel(attn_ref, y_ref, sa_ref, sb_ref, x_ref, wsb_ref, wcv_ref, wout_ref, g2_ref,
                  wrh_ref, wrl_ref, br_ref, upper_ref, x1t_ref, route_ref, cnt_ref, cnt_scr):
    tm = x_ref.shape[0]
    branch_a = _dot(attn_ref[...], wsb_ref[...])
    branch_b = _dot(y_ref[...], wcv_ref[...])
    merged = (sa_ref[...] * branch_a + sb_ref[...] * branch_b).astype(BF16)
    x1 = x_ref[...] + _dot(merged, wout_ref[...])
    x1t_ref[...] = x1.reshape(tm, SUBLANES, LANES)

    logits = _router_logits(wrh_ref, wrl_ref, br_ref, _ffn_norm(x1, g2_ref))
    rowi = lax.broadcasted_iota(jnp.int32, (SUBLANES, tm), 0).astype(F32)
    gl = jnp.where(rowi < N_GROUPS, logits[0:SUBLANES, :], jnp.float32(-jnp.inf))
    _, gidx = _first_argmax(gl, rowi)

    @pl.when(pl.program_id(0) == 0)
    def _():
        cnt_scr[...] = jnp.zeros_like(cnt_scr)

    onehot = jnp.where(rowi == gidx, 1.0, 0.0)
    before = _dot(onehot.astype(BF16), upper_ref[...]) + cnt_scr[:, 0:1]
    rank = jnp.sum(onehot * before, axis=0, keepdims=True)
    cnt_scr[...] = cnt_scr[...] + jnp.sum(onehot, axis=1, keepdims=True)
    cnt_ref[...] = cnt_scr[...]
    route_ref[...] = jnp.where(rowi == 0, gidx, jnp.where(rowi == 1, rank, 0.0))


def _merge_call(attn, y, sa, sb, x2d, wsb, wcv, wout, g2, wrh, wrl, br, upper):
    n = x2d.shape[0]
    tm = TM_PROJ
    rows = lambda width: pl.BlockSpec((tm, width), lambda i: (i, 0))
    const = lambda shape: pl.BlockSpec(shape, lambda i: (0,) * len(shape),
                                       pipeline_mode=pl.Buffered(1))
    slab = (1, D_MODEL, LANES)
    return pl.pallas_call(
        _merge_kernel,
        grid=(n // tm,),
        in_specs=[rows(WIDTH), rows(WIDTH), rows(D_MODEL), rows(D_MODEL), rows(D_MODEL),
                  const((WIDTH, D_MODEL)), const((WIDTH, D_MODEL)), const((D_MODEL, D_MODEL)),
                  const((1, D_MODEL)), const(slab), const(slab), const((1, 1, LANES)),
                  const((tm, tm))],
        out_specs=[pl.BlockSpec((tm, SUBLANES, LANES), lambda i: (i, 0, 0)),
                   pl.BlockSpec((SUBLANES, tm), lambda i: (0, i)),
                   pl.BlockSpec((SUBLANES, LANES), lambda i: (0, 0))],
        out_shape=[jax.ShapeDtypeStruct((n, SUBLANES, LANES), F32),
                   jax.ShapeDtypeStruct((SUBLANES, n), F32),
                   jax.ShapeDtypeStruct((SUBLANES, LANES), F32)],
        scratch_shapes=[pltpu.VMEM((SUBLANES, LANES), F32)],
        compiler_params=pltpu.CompilerParams(dimension_semantics=("arbitrary",),
                                             vmem_limit_bytes=VMEM_LIMIT),
        name="merge_router",
    )(attn, y, sa, sb, x2d, wsb, wcv, wout, g2, wrh, wrl, br, upper)


def _row_copy(src_ref, src_row, dst_ref, dst_row, sem):
    return pltpu.make_async_copy(src_ref.at[src_row], dst_ref.at[dst_row], sem)


def _scatter_kernel(dest_ref, fill_start_ref, fill_n_ref, x_ref, xs_ref, zero_scr, sem):
    i = pl.program_id(0)
    tm = x_ref.shape[0]
    base = i * tm

    def start(r, carry):
        _row_copy(x_ref, r, xs_ref, dest_ref[base + r], sem).start()
        return carry

    lax.fori_loop(0, tm, start, 0, unroll=ISSUE_UNROLL)
    pltpu.make_async_copy(x_ref, xs_ref.at[pl.ds(0, tm)], sem).wait()

    @pl.when(i == pl.num_programs(0) - 1)
    def _():
        zero_scr[...] = jnp.zeros_like(zero_scr)
        for g in range(N_GROUPS + 1):
            first = fill_start_ref[g]

            def start_fill(r, carry):
                _row_copy(zero_scr, 0, xs_ref, first + r, sem).start()
                return carry

            def wait_fill(r, carry):
                _row_copy(zero_scr, 0, xs_ref, 0, sem).wait()
                return carry

            lax.fori_loop(0, fill_n_ref[g], start_fill, 0)
            lax.fori_loop(0, fill_n_ref[g], wait_fill, 0)


def _scatter_call(dest, fill_start, fill_n, x1t, p_rows):
    n = x1t.shape[0]
    tm = TM_PERM
    grid_spec = pltpu.PrefetchScalarGridSpec(
        num_scalar_prefetch=3,
        grid=(n // tm,),
        in_specs=[pl.BlockSpec((tm, SUBLANES, LANES), lambda i, *_: (i, 0, 0))],
        out_specs=pl.BlockSpec(memory_space=pl.ANY),
        scratch_shapes=[pltpu.VMEM((1, SUBLANES, LANES), F32), pltpu.SemaphoreType.DMA(())],
    )
    return pl.pallas_call(
        _scatter_kernel,
        grid_spec=grid_spec,
        out_shape=jax.ShapeDtypeStruct((p_rows, SUBLANES, LANES), F32),
        compiler_params=pltpu.CompilerParams(dimension_semantics=("arbitrary",),
                                             vmem_limit_bytes=VMEM_LIMIT),
        name="row_scatter",
    )(dest, fill_start, fill_n, x1t)


def _moe_kernel(tg_ref, tv_ref, xs_ref, g2_ref, wrh_ref, wrl_ref, br_ref, wg_ref, wu_ref, wd_ref,
                o_ref):
    i = pl.program_id(0)
    t = xs_ref.shape[0]

    @pl.when(tv_ref[i] == 0)
    def _():
        o_ref[...] = jnp.zeros_like(o_ref)

    @pl.when(tv_ref[i] > 0)
    def _():
        x1 = xs_ref[...].reshape(t, D_MODEL)
        h2 = _ffn_norm(x1, g2_ref)

        rowi = lax.broadcasted_iota(jnp.int32, (SUBLANES, t), 0).astype(F32)
        neg = jnp.float32(-jnp.inf)
        logits = _router_logits(wrh_ref, wrl_ref, br_ref, h2)
        gl = jnp.where(rowi < N_GROUPS, logits[0:SUBLANES, :], neg)
        mine = jnp.sum(jnp.where(rowi == tg_ref[i].astype(F32), gl, 0.0), axis=0, keepdims=True)
        g_w = 1.0 / jnp.sum(jnp.exp(gl - mine), axis=0, keepdims=True)
        el = logits[SUBLANES:2 * SUBLANES, :]
        t1, i1 = _first_argmax(el, rowi)
        t2, i2 = _first_argmax(jnp.where(rowi == i1, neg, el), rowi)
        e = jnp.exp(t2 - t1)
        w1 = g_w / (1.0 + e)
        w2 = g_w * e / (1.0 + e)
        cw_t = jnp.where(rowi == i1, w1, 0.0) + jnp.where(rowi == i2, w2, 0.0)
        cw = jnp.concatenate([cw_t, jnp.zeros((LANES - SUBLANES, t), F32)], axis=0).T

        hb = h2.astype(BF16)
        gate_up = lambda ex: (_dot(hb, wg_ref[ex]), _dot(hb, wu_ref[ex]))
        acc = x1
        ahead = gate_up(0)
        for ex in range(E_PER_GROUP):
            gate, up = ahead
            if ex + 1 < E_PER_GROUP:
                ahead = gate_up(ex + 1)
            act = (gate * jax.nn.sigmoid(gate)) * up * cw[:, ex:ex + 1]
            acc = acc + _dot(act.astype(BF16), wd_ref[ex])
        o_ref[...] = acc.reshape(t, SUBLANES, LANES)


def _moe_call(tile_group, tile_valid, xs, g2, wrh, wrl, br, wg, wu, wd):
    p = xs.shape[0]
    t = TM_MOE
    by_block = lambda i, tg, tv: (i, 0, 0)
    by_group = lambda i, tg, tv: (tg[i], 0, 0)
    once = pl.Buffered(1)
    grid_spec = pltpu.PrefetchScalarGridSpec(
        num_scalar_prefetch=2,
        grid=(p // t,),
        in_specs=[pl.BlockSpec((t, SUBLANES, LANES), by_block),
                  pl.BlockSpec((1, D_MODEL), lambda i, *_: (0, 0)),
                  pl.BlockSpec((1, D_MODEL, LANES), by_group, pipeline_mode=once),
                  pl.BlockSpec((1, D_MODEL, LANES), by_group, pipeline_mode=once),
                  pl.BlockSpec((1, 1, LANES), by_group, pipeline_mode=once),
                  pl.BlockSpec((E_PER_GROUP, D_MODEL, D_EXPERT), by_group, pipeline_mode=once),
                  pl.BlockSpec((E_PER_GROUP, D_MODEL, D_EXPERT), by_group, pipeline_mode=once),
                  pl.BlockSpec((E_PER_GROUP, D_EXPERT, D_MODEL), by_group, pipeline_mode=once)],
        out_specs=pl.BlockSpec((t, SUBLANES, LANES), by_block),
    )
    return pl.pallas_call(
        _moe_kernel,
        grid_spec=grid_spec,
        out_shape=jax.ShapeDtypeStruct((p, SUBLANES, LANES), F32),
        compiler_params=pltpu.CompilerParams(dimension_semantics=("arbitrary",),
                                             vmem_limit_bytes=VMEM_LIMIT),
        name="moe_experts",
    )(tile_group, tile_valid, xs, g2, wrh, wrl, br, wg, wu, wd)


def _gather_kernel(dest_ref, ys_ref, o_ref, rows_scr, sem):
    i = pl.program_id(0)
    tm = o_ref.shape[0]
    base = i * tm

    def start(r, carry):
        _row_copy(ys_ref, dest_ref[base + r], rows_scr, r, sem).start()
        return carry

    lax.fori_loop(0, tm, start, 0, unroll=ISSUE_UNROLL)
    pltpu.make_async_copy(ys_ref.at[pl.ds(0, tm)], rows_scr, sem).wait()
    o_ref[...] = rows_scr[...].reshape(tm, D_MODEL)


def _gather_call(dest, ys, n):
    tm = TM_PERM
    grid_spec = pltpu.PrefetchScalarGridSpec(
        num_scalar_prefetch=1,
        grid=(n // tm,),
        in_specs=[pl.BlockSpec(memory_space=pl.ANY)],
        out_specs=pl.BlockSpec((tm, D_MODEL), lambda i, *_: (i, 0)),
        scratch_shapes=[pltpu.VMEM((tm, SUBLANES, LANES), F32), pltpu.SemaphoreType.DMA(())],
    )
    return pl.pallas_call(
        _gather_kernel,
        grid_spec=grid_spec,
        out_shape=jax.ShapeDtypeStruct((n, D_MODEL), F32),
        compiler_params=pltpu.CompilerParams(dimension_semantics=("arbitrary",),
                                             vmem_limit_bytes=VMEM_LIMIT),
        name="row_gather",
    )(dest, ys)


def _router_slabs(w_rg, b_rg, w_re, b_re):
    w = jnp.zeros((N_GROUPS, D_MODEL, LANES), F32)
    w = w.at[:, :, 0:N_GROUPS].set(w_rg)
    w = w.at[:, :, SUBLANES:SUBLANES + E_PER_GROUP].set(
        w_re.reshape(D_MODEL, N_GROUPS, E_PER_GROUP).transpose(1, 0, 2))
    b = jnp.zeros((N_GROUPS, 1, LANES), F32)
    b = b.at[:, 0, 0:N_GROUPS].set(b_rg)
    b = b.at[:, 0, SUBLANES:SUBLANES + E_PER_GROUP].set(b_re.reshape(N_GROUPS, E_PER_GROUP))
    hi = w.astype(BF16)
    lo = (w - hi.astype(F32)).astype(BF16)
    return hi, lo, b


def _layer(x2d, batch, seq_len, norm_mix_g, w_in, q_norm_g, k_norm_g, conv_w, w_sb_branch,
           w_conv_branch, w_out, norm_ffn_g, w_router_group, b_router_group, w_router_expert,
           b_router_expert, w_gate_e, w_up_e, w_down_e):
    n = x2d.shape[0]
    lane = jnp.arange(WIDTH)
    head_mean = jnp.where(lane[:, None] // HEAD_DIM == lane[None, :] // HEAD_DIM,
                          1.0 / HEAD_DIM, 0.0).astype(BF16)
    qg = (jnp.tile(q_norm_g, N_HEADS) * HEAD_DIM ** -0.5)[None, :]
    kg = jnp.tile(k_norm_g, N_HEADS)[None, :]
    qt, k, vt, y, sa, sb = _inproj_call(x2d, norm_mix_g[None, :], w_in.astype(BF16), qg, kg,
                                        head_mean, conv_w, seq_len)

    pos = jnp.arange(TK)
    tri = (pos[None, :] >= pos[:, None]).astype(BF16)
    attn, (wg16, wu16, wd16) = _attn_call(qt, k, vt, tri, (w_gate_e, w_up_e, w_down_e), batch,
                                          seq_len)

    wrh, wrl, br = _router_slabs(w_router_group, b_router_group, w_router_expert, b_router_expert)
    tpos = jnp.arange(TM_PROJ)
    upper = (tpos[:, None] < tpos[None, :]).astype(BF16)
    x1t, route, cnt = _merge_call(attn, y, sa, sb, x2d, w_sb_branch.astype(BF16),
                                  w_conv_branch.astype(BF16), w_out.astype(BF16),
                                  norm_ffn_g[None, :], wrh, wrl, br, upper)

    t = TM_MOE
    grp = route[0].astype(jnp.int32)
    rank = route[1].astype(jnp.int32)
    counts = cnt[0:N_GROUPS, 0].astype(jnp.int32)
    padded = ((counts + t - 1) // t) * t
    ends = jnp.cumsum(padded)
    offs = ends - padded
    dest = rank
    for g in range(N_GROUPS):
        dest = dest + jnp.where(grp == g, offs[g], 0)
    n_tiles = n // t + N_GROUPS
    tile_start = jnp.arange(n_tiles, dtype=jnp.int32) * t
    tile_valid = (tile_start < ends[-1]).astype(jnp.int32)
    tile_group = jnp.zeros_like(tile_start)
    for g in range(N_GROUPS - 1):
        tile_group = tile_group + (jnp.minimum(tile_start, ends[-1] - t) >= ends[g]).astype(jnp.int32)
    p_rows = n_tiles * t
    fill_start = jnp.concatenate([offs + counts, ends[-1:]])
    fill_n = jnp.concatenate([padded - counts, p_rows - ends[-1:]])

    xs = _scatter_call(dest, fill_start, fill_n, x1t, p_rows)
    ys = _moe_call(tile_group, tile_valid, xs, norm_ffn_g[None, :], wrh, wrl, br, wg16, wu16, wd16)
    return _gather_call(dest, ys, n)


def kernel(x, norm_mix_g, w_in, q_norm_g, k_norm_g, conv_w, w_sb_branch, w_conv_branch, w_out,
           norm_ffn_g, w_router_group, b_router_group, w_router_expert, b_router_expert,
           w_gate_e, w_up_e, w_down_e):
    batch, seq_len, d = x.shape
    x2d = x.reshape(batch * seq_len, d)
    for l in range(norm_mix_g.shape[0]):
        x2d = _layer(x2d, batch, seq_len, norm_mix_g[l], w_in[l], q_norm_g[l], k_norm_g[l],
                     conv_w[l], w_sb_branch[l], w_conv_branch[l], w_out[l], norm_ffn_g[l],
                     w_router_group[l], b_router_group[l], w_router_expert[l], b_router_expert[l],
                     w_gate_e[l], w_up_e[l], w_down_e[l])
    return x2d.reshape(batch, seq_len, d)
```

```python
import functools

import jax
import jax.numpy as jnp
from jax import lax
from jax.experimental import pallas as pl
from jax.experimental.pallas import tpu as pltpu

F32 = jnp.float32
BF16 = jnp.bfloat16

D_MODEL = 1024
N_HEADS = 8
HEAD_DIM = 64
WIDTH = N_HEADS * HEAD_DIM
CONV_K = 3
N_GROUPS = 4
E_PER_GROUP = 8
N_EXPERTS = N_GROUPS * E_PER_GROUP
D_EXPERT = 256
EPS = 1e-6

LANES = 128
SUBLANES = 8
HEADS_PER_STEP = LANES // HEAD_DIM
N_PAIRS = N_HEADS // HEADS_PER_STEP

TM_PROJ = 512
TQ = 256
TK = 256
TM_MOE = 512
TM_PERM = 512
ISSUE_UNROLL = 8
VMEM_LIMIT = 56 * 1024 * 1024

def _dot(a, b):
    return jnp.dot(a, b, preferred_element_type=F32)


def _split_bf16(v):
    hi = v.astype(BF16)
    lo = (v - hi.astype(F32)).astype(BF16)
    return hi, lo


def _inproj_kernel(x_ref, g1_ref, w_ref, qg_ref, kg_ref, pm_ref, cw_ref,
                   qt_ref, k_ref, vt_ref, y_ref, sa_ref, sb_ref, cu_scr, *, tiles_per_seq):
    tm = x_ref.shape[0]
    x = x_ref[...]
    ms = jnp.mean(x * x, axis=-1, keepdims=True)
    h = (x * lax.rsqrt(ms + EPS) * g1_ref[...]).astype(BF16)

    def proj(lo, width):
        return _dot(h, w_ref[:, lo:lo + width])

    def head_norm(t, g_ref):
        hms = _dot((t * t).astype(BF16), pm_ref[...])
        return t * lax.rsqrt(hms + EPS) * g_ref[...]

    q = head_norm(proj(0, WIDTH), qg_ref)
    v = proj(2 * WIDTH, WIDTH)
    for blk in range(tm // TQ):
        qt_ref[blk] = q[blk * TQ:(blk + 1) * TQ, :].T.astype(BF16)
        vt_ref[blk] = v[blk * TK:(blk + 1) * TK, :].T.astype(BF16)
    k_ref[...] = head_norm(proj(WIDTH, WIDTH), kg_ref).astype(BF16)

    c_b = proj(3 * WIDTH, WIDTH)
    cu = proj(4 * WIDTH, WIDTH) * proj(5 * WIDTH, WIDTH)

    @pl.when(pl.program_id(0) % tiles_per_seq == 0)
    def _():
        cu_scr[0:SUBLANES, :] = jnp.zeros((SUBLANES, WIDTH), F32)

    cu_scr[SUBLANES:SUBLANES + tm, :] = cu
    cw = cw_ref[...]
    conv = (cw[0:1, :] * cu_scr[SUBLANES - 2:SUBLANES - 2 + tm, :]
            + cw[1:2, :] * cu_scr[SUBLANES - 1:SUBLANES - 1 + tm, :]
            + cw[2:3, :] * cu)
    y_ref[...] = (c_b * conv).astype(BF16)
    cu_scr[0:SUBLANES, :] = cu_scr[tm:tm + SUBLANES, :]

    sa_ref[...] = jax.nn.sigmoid(proj(6 * WIDTH, D_MODEL))
    sb_ref[...] = jax.nn.sigmoid(proj(6 * WIDTH + D_MODEL, D_MODEL))


def _inproj_call(x2d, g1, w_in, qg, kg, pm, conv_w, seq_len):
    n = x2d.shape[0]
    tm = TM_PROJ
    in_w = w_in.shape[1]
    const = lambda shape: pl.BlockSpec(shape, lambda i: (0, 0), pipeline_mode=pl.Buffered(1))
    rows = lambda width: pl.BlockSpec((tm, width), lambda i: (i, 0))
    assert TQ == TK and tm % TQ == 0
    cols = pl.BlockSpec((tm // TQ, WIDTH, TQ), lambda i: (i, 0, 0))
    cols_shape = jax.ShapeDtypeStruct((n // TQ, WIDTH, TQ), BF16)
    return pl.pallas_call(
        functools.partial(_inproj_kernel, tiles_per_seq=seq_len // tm),
        grid=(n // tm,),
        in_specs=[rows(D_MODEL), const((1, D_MODEL)), const((D_MODEL, in_w)), const((1, WIDTH)),
                  const((1, WIDTH)), const((WIDTH, WIDTH)), const((CONV_K, WIDTH))],
        out_specs=[cols, rows(WIDTH), cols, rows(WIDTH), rows(D_MODEL), rows(D_MODEL)],
        out_shape=[cols_shape, jax.ShapeDtypeStruct((n, WIDTH), BF16)] * 2
                  + [jax.ShapeDtypeStruct((n, D_MODEL), F32)] * 2,
        scratch_shapes=[pltpu.VMEM((SUBLANES + tm, WIDTH), F32)],
        compiler_params=pltpu.CompilerParams(dimension_semantics=("arbitrary",),
                                             vmem_limit_bytes=VMEM_LIMIT),
        name="inproj",
    )(x2d, g1, w_in, qg, kg, pm, conv_w)


def _softplus(z):
    return jnp.maximum(z, 0.0) + jnp.log(1.0 + jnp.exp(-jnp.abs(z)))


def _attn_kernel(qt_ref, k_ref, vt_ref, tri_ref, wg_ref, wu_ref, wd_ref, o_ref, wg16_ref, wu16_ref,
                 wd16_ref, qm_scr, z_scr, zc_scr, suf_scr, a_scr, acc_scr, c_scr):
    i = pl.program_id(1)
    tq = qt_ref.shape[2]
    feat = lax.broadcasted_iota(jnp.int32, (LANES, tq), 0)
    first_head = feat < HEAD_DIM
    for p in range(N_PAIRS):
        q2 = qt_ref[0, p * LANES:(p + 1) * LANES, :]
        zero = jnp.zeros_like(q2)
        qm_scr[2 * p] = jnp.where(first_head, q2, zero)
        qm_scr[2 * p + 1] = jnp.where(first_head, zero, q2)
    acc_scr[...] = jnp.zeros_like(acc_scr)
    c_scr[...] = jnp.zeros_like(c_scr)

    def pair(s):
        p = s // HEADS_PER_STEP
        return slice(p * LANES, (p + 1) * LANES)

    def scores(s, j):
        keys = k_ref[pl.ds(pl.multiple_of(j * TK, TK), TK), pair(s)]
        z_scr[s] = _dot(keys, qm_scr[s])

    def suffix_sums(s, mask):
        z = z_scr[s]
        zc_scr[s] = z - c_scr[s]
        sp = _softplus(z.astype(BF16))
        if mask is not None:
            sp = jnp.where(mask, sp, jnp.zeros_like(sp))
        suf_scr[s] = _dot(tri_ref[...], sp)

    def weights(s, mask):
        suffix = suf_scr[s]
        a = jnp.exp(zc_scr[s] - suffix)
        if mask is not None:
            a = jnp.where(mask, a, 0.0)
        a_scr[s] = a.astype(BF16)
        c_scr[s] = c_scr[s] + suffix[0:1, :]

    def accumulate(s, j):
        acc_scr[s] = acc_scr[s] + _dot(vt_ref[j, pair(s), :], a_scr[s])

    key = lax.broadcasted_iota(jnp.int32, (TK, tq), 0)
    qry = lax.broadcasted_iota(jnp.int32, (TK, tq), 1)
    causal = key < qry
    nxt = jnp.maximum(i - 1, 0)
    for s in range(N_HEADS):
        scores(s, i)
    for s in range(N_HEADS):
        suffix_sums(s, causal)
        scores(s, nxt)
    for s in range(N_HEADS):
        weights(s, causal)

    def body(t, carry):
        j = i - t
        nxt = jnp.maximum(j - 1, 0)
        for s in range(N_HEADS):
            accumulate(s, j + 1)
            suffix_sums(s, None)
            scores(s, nxt)
        for s in range(N_HEADS):
            weights(s, None)
        return carry

    lax.fori_loop(1, i + 1, body, 0)
    for s in range(N_HEADS):
        accumulate(s, 0)
    for p in range(N_PAIRS):
        both = jnp.where(first_head, acc_scr[2 * p], acc_scr[2 * p + 1])
        o_ref[:, p * LANES:(p + 1) * LANES] = both.T.astype(o_ref.dtype)

    wg16_ref[...] = wg_ref[...].astype(BF16)
    wu16_ref[...] = wu_ref[...].astype(BF16)
    wd16_ref[...] = wd_ref[...].astype(BF16)


def _attn_call(qt, k, vt, tri, expert_weights, batch, seq_len):
    assert TQ == TK
    nq = seq_len // TQ
    steps = batch * nq
    qspec = pl.BlockSpec((1, WIDTH, TQ), lambda b, i: (b * nq + i, 0, 0))
    kspec = pl.BlockSpec((seq_len, WIDTH), lambda b, i: (b, 0))
    vspec = pl.BlockSpec((seq_len // TK, WIDTH, TK), lambda b, i: (b, 0, 0))
    ospec = pl.BlockSpec((TQ, WIDTH), lambda b, i: (b * nq + i, 0))
    slabs = [w.reshape(steps, -1, w.shape[-1]) for w in expert_weights]
    wspecs = [pl.BlockSpec((1,) + s.shape[1:], lambda b, i: (b * nq + i, 0, 0)) for s in slabs]
    attn, *cast = pl.pallas_call(
        _attn_kernel,
        grid=(batch, nq),
        in_specs=[qspec, kspec, vspec, pl.BlockSpec((TK, TK), lambda b, i: (0, 0))] + wspecs,
        out_specs=[ospec] + wspecs,
        out_shape=[jax.ShapeDtypeStruct(k.shape, BF16)]
                  + [jax.ShapeDtypeStruct(s.shape, BF16) for s in slabs],
        scratch_shapes=[pltpu.VMEM((N_HEADS, LANES, TQ), BF16),
                        pltpu.VMEM((N_HEADS, TK, TQ), F32),
                        pltpu.VMEM((N_HEADS, TK, TQ), F32),
                        pltpu.VMEM((N_HEADS, TK, TQ), F32),
                        pltpu.VMEM((N_HEADS, TK, TQ), BF16),
                        pltpu.VMEM((N_HEADS, LANES, TQ), F32),
                        pltpu.VMEM((N_HEADS, 1, TQ), F32)],
        compiler_params=pltpu.CompilerParams(dimension_semantics=("arbitrary", "arbitrary"),
                                             vmem_limit_bytes=VMEM_LIMIT),
        name="sb_attn",
    )(qt, k, vt, tri, *slabs)
    return attn, [c.reshape(w.shape) for c, w in zip(cast, expert_weights)]


def _router_logits(w_hi_ref, w_lo_ref, b_ref, h):
    hh, hl = _split_bf16(h)
    logits = _dot(hh, w_hi_ref[0]) + _dot(hl, w_hi_ref[0]) + _dot(hh, w_lo_ref[0]) + b_ref[0]
    return logits.T


def _first_argmax(vals, rowi):
    top = jnp.max(vals, axis=0, keepdims=True)
    idx = jnp.min(jnp.where(vals == top, rowi, jnp.float32(SUBLANES)), axis=0, keepdims=True)
    return top, idx


def _ffn_norm(x1, g2_ref):
    ms = jnp.mean(x1 * x1, axis=-1, keepdims=True)
    return x1 * lax.rsqrt(ms + EPS) * g2_ref[...]


def _merge_kernel(attn_ref, y_ref, sa_ref, sb_ref, x_ref, wsb_ref, wcv_ref, wout_ref, g2_ref,
                  wrh_ref, wrl_ref, br_ref, upper_ref, x1t_ref, route_ref, cnt_ref, cnt_scr):
    tm = x_ref.shape[0]
    branch_a = _dot(attn_ref[...], wsb_ref[...])
    branch_b = _dot(y_ref[...], wcv_ref[...])
    merged = (sa_ref[...] * branch_a + sb_ref[...] * branch_b).astype(BF16)
    x1 = x_ref[...] + _dot(merged, wout_ref[...])
    x1t_ref[...] = x1.reshape(tm, SUBLANES, LANES)

    logits = _router_logits(wrh_ref, wrl_ref, br_ref, _ffn_norm(x1, g2_ref))
    rowi = lax.broadcasted_iota(jnp.int32, (SUBLANES, tm), 0).astype(F32)
    gl = jnp.where(rowi < N_GROUPS, logits[0:SUBLANES, :], jnp.float32(-jnp.inf))
    _, gidx = _first_argmax(gl, rowi)

    @pl.when(pl.program_id(0) == 0)
    def _():
        cnt_scr[...] = jnp.zeros_like(cnt_scr)

    onehot = jnp.where(rowi == gidx, 1.0, 0.0)
    before = _dot(onehot.astype(BF16), upper_ref[...]) + cnt_scr[:, 0:1]
    rank = jnp.sum(onehot * before, axis=0, keepdims=True)
    cnt_scr[...] = cnt_scr[...] + jnp.sum(onehot, axis=1, keepdims=True)
    cnt_ref[...] = cnt_scr[...]
    route_ref[...] = jnp.where(rowi == 0, gidx, jnp.where(rowi == 1, rank, 0.0))


def _merge_call(attn, y, sa, sb, x2d, wsb, wcv, wout, g2, wrh, wrl, br, upper):
    n = x2d.shape[0]
    tm = TM_PROJ
    rows = lambda width: pl.BlockSpec((tm, width), lambda i: (i, 0))
    const = lambda shape: pl.BlockSpec(shape, lambda i: (0,) * len(shape),
                                       pipeline_mode=pl.Buffered(1))
    slab = (1, D_MODEL, LANES)
    return pl.pallas_call(
        _merge_kernel,
        grid=(n // tm,),
        in_specs=[rows(WIDTH), rows(WIDTH), rows(D_MODEL), rows(D_MODEL), rows(D_MODEL),
                  const((WIDTH, D_MODEL)), const((WIDTH, D_MODEL)), const((D_MODEL, D_MODEL)),
                  const((1, D_MODEL)), const(slab), const(slab), const((1, 1, LANES)),
                  const((tm, tm))],
        out_specs=[pl.BlockSpec((tm, SUBLANES, LANES), lambda i: (i, 0, 0)),
                   pl.BlockSpec((SUBLANES, tm), lambda i: (0, i)),
                   pl.BlockSpec((SUBLANES, LANES), lambda i: (0, 0))],
        out_shape=[jax.ShapeDtypeStruct((n, SUBLANES, LANES), F32),
                   jax.ShapeDtypeStruct((SUBLANES, n), F32),
                   jax.ShapeDtypeStruct((SUBLANES, LANES), F32)],
        scratch_shapes=[pltpu.VMEM((SUBLANES, LANES), F32)],
        compiler_params=pltpu.CompilerParams(dimension_semantics=("arbitrary",),
                                             vmem_limit_bytes=VMEM_LIMIT),
        name="merge_router",
    )(attn, y, sa, sb, x2d, wsb, wcv, wout, g2, wrh, wrl, br, upper)


def _row_copy(src_ref, src_row, dst_ref, dst_row, sem):
    return pltpu.make_async_copy(src_ref.at[src_row], dst_ref.at[dst_row], sem)


def _scatter_kernel(dest_ref, fill_start_ref, fill_n_ref, x_ref, xs_ref, zero_scr, sem):
    i = pl.program_id(0)
    tm = x_ref.shape[0]
    base = i * tm

    def start(k, carry):
        for u in range(ISSUE_UNROLL):
            r = k * ISSUE_UNROLL + u
            _row_copy(x_ref, r, xs_ref, dest_ref[base + r], sem).start(priority=u % 2)
        return carry

    lax.fori_loop(0, tm // ISSUE_UNROLL, start, 0)
    pltpu.make_async_copy(x_ref, xs_ref.at[pl.ds(0, tm)], sem).wait()

    @pl.when(i == pl.num_programs(0) - 1)
    def _():
        zero_scr[...] = jnp.zeros_like(zero_scr)
        for g in range(N_GROUPS + 1):
            first = fill_start_ref[g]

            def start_fill(r, carry):
                _row_copy(zero_scr, 0, xs_ref, first + r, sem).start()
                return carry

            def wait_fill(r, carry):
                _row_copy(zero_scr, 0, xs_ref, 0, sem).wait()
                return carry

            lax.fori_loop(0, fill_n_ref[g], start_fill, 0)
            lax.fori_loop(0, fill_n_ref[g], wait_fill, 0)


def _scatter_call(dest, fill_start, fill_n, x1t, p_rows):
    n = x1t.shape[0]
    tm = TM_PERM
    grid_spec = pltpu.PrefetchScalarGridSpec(
        num_scalar_prefetch=3,
        grid=(n // tm,),
        in_specs=[pl.BlockSpec((tm, SUBLANES, LANES), lambda i, *_: (i, 0, 0))],
        out_specs=pl.BlockSpec(memory_space=pl.ANY),
        scratch_shapes=[pltpu.VMEM((1, SUBLANES, LANES), F32), pltpu.SemaphoreType.DMA(())],
    )
    return pl.pallas_call(
        _scatter_kernel,
        grid_spec=grid_spec,
        out_shape=jax.ShapeDtypeStruct((p_rows, SUBLANES, LANES), F32),
        compiler_params=pltpu.CompilerParams(dimension_semantics=("arbitrary",),
                                             vmem_limit_bytes=VMEM_LIMIT),
        name="row_scatter",
    )(dest, fill_start, fill_n, x1t)


def _moe_kernel(tg_ref, tv_ref, xs_ref, g2_ref, wrh_ref, wrl_ref, br_ref, wg_ref, wu_ref, wd_ref,
                o_ref):
    i = pl.program_id(0)
    t = xs_ref.shape[0]

    @pl.when(tv_ref[i] == 0)
    def _():
        o_ref[...] = jnp.zeros_like(o_ref)

    @pl.when(tv_ref[i] > 0)
    def _():
        x1 = xs_ref[...].reshape(t, D_MODEL)
        h2 = _ffn_norm(x1, g2_ref)

        rowi = lax.broadcasted_iota(jnp.int32, (SUBLANES, t), 0).astype(F32)
        neg = jnp.float32(-jnp.inf)
        logits = _router_logits(wrh_ref, wrl_ref, br_ref, h2)
        gl = jnp.where(rowi < N_GROUPS, logits[0:SUBLANES, :], neg)
        mine = jnp.sum(jnp.where(rowi == tg_ref[i].astype(F32), gl, 0.0), axis=0, keepdims=True)
        g_w = 1.0 / jnp.sum(jnp.exp(gl - mine), axis=0, keepdims=True)
        el = logits[SUBLANES:2 * SUBLANES, :]
        t1, i1 = _first_argmax(el, rowi)
        t2, i2 = _first_argmax(jnp.where(rowi == i1, neg, el), rowi)
        e = jnp.exp(t2 - t1)
        w1 = g_w / (1.0 + e)
        w2 = g_w * e / (1.0 + e)
        cw_t = jnp.where(rowi == i1, w1, 0.0) + jnp.where(rowi == i2, w2, 0.0)
        cw = jnp.concatenate([cw_t, jnp.zeros((LANES - SUBLANES, t), F32)], axis=0).T

        hb = h2.astype(BF16)
        gate_up = lambda ex: (_dot(hb, wg_ref[ex]), _dot(hb, wu_ref[ex]))
        acc = x1
        ahead = gate_up(0)
        for ex in range(E_PER_GROUP):
            gate, up = ahead
            if ex + 1 < E_PER_GROUP:
                ahead = gate_up(ex + 1)
            act = (gate * jax.nn.sigmoid(gate)) * up * cw[:, ex:ex + 1]
            acc = acc + _dot(act.astype(BF16), wd_ref[ex])
        o_ref[...] = acc.reshape(t, SUBLANES, LANES)


def _moe_call(tile_group, tile_valid, xs, g2, wrh, wrl, br, wg, wu, wd):
    p = xs.shape[0]
    t = TM_MOE
    by_block = lambda i, tg, tv: (i, 0, 0)
    by_group = lambda i, tg, tv: (tg[i], 0, 0)
    once = pl.Buffered(1)
    grid_spec = pltpu.PrefetchScalarGridSpec(
        num_scalar_prefetch=2,
        grid=(p // t,),
        in_specs=[pl.BlockSpec((t, SUBLANES, LANES), by_block),
                  pl.BlockSpec((1, D_MODEL), lambda i, *_: (0, 0)),
                  pl.BlockSpec((1, D_MODEL, LANES), by_group, pipeline_mode=once),
                  pl.BlockSpec((1, D_MODEL, LANES), by_group, pipeline_mode=once),
                  pl.BlockSpec((1, 1, LANES), by_group, pipeline_mode=once),
                  pl.BlockSpec((E_PER_GROUP, D_MODEL, D_EXPERT), by_group, pipeline_mode=once),
                  pl.BlockSpec((E_PER_GROUP, D_MODEL, D_EXPERT), by_group, pipeline_mode=once),
                  pl.BlockSpec((E_PER_GROUP, D_EXPERT, D_MODEL), by_group, pipeline_mode=once)],
        out_specs=pl.BlockSpec((t, SUBLANES, LANES), by_block),
    )
    return pl.pallas_call(
        _moe_kernel,
        grid_spec=grid_spec,
        out_shape=jax.ShapeDtypeStruct((p, SUBLANES, LANES), F32),
        compiler_params=pltpu.CompilerParams(dimension_semantics=("arbitrary",),
                                             vmem_limit_bytes=VMEM_LIMIT),
        name="moe_experts",
    )(tile_group, tile_valid, xs, g2, wrh, wrl, br, wg, wu, wd)


def _gather_kernel(dest_ref, ys_ref, o_ref, rows_scr, sem):
    i = pl.program_id(0)
    tm = o_ref.shape[0]
    base = i * tm

    def start(k, carry):
        for u in range(ISSUE_UNROLL):
            r = k * ISSUE_UNROLL + u
            _row_copy(ys_ref, dest_ref[base + r], rows_scr, r, sem).start(priority=u % 2)
        return carry

    lax.fori_loop(0, tm // ISSUE_UNROLL, start, 0)
    pltpu.make_async_copy(ys_ref.at[pl.ds(0, tm)], rows_scr, sem).wait()
    o_ref[...] = rows_scr[...].reshape(tm, D_MODEL)


def _gather_call(dest, ys, n):
    tm = TM_PERM
    grid_spec = pltpu.PrefetchScalarGridSpec(
        num_scalar_prefetch=1,
        grid=(n // tm,),
        in_specs=[pl.BlockSpec(memory_space=pl.ANY)],
        out_specs=pl.BlockSpec((tm, D_MODEL), lambda i, *_: (i, 0)),
        scratch_shapes=[pltpu.VMEM((tm, SUBLANES, LANES), F32), pltpu.SemaphoreType.DMA(())],
    )
    return pl.pallas_call(
        _gather_kernel,
        grid_spec=grid_spec,
        out_shape=jax.ShapeDtypeStruct((n, D_MODEL), F32),
        compiler_params=pltpu.CompilerParams(dimension_semantics=("arbitrary",),
                                             vmem_limit_bytes=VMEM_LIMIT),
        name="row_gather",
    )(dest, ys)


def _router_slabs(w_rg, b_rg, w_re, b_re):
    def slabs(group_part, expert_part):
        lead = expert_part.shape[:-1]
        pads = [jnp.zeros(lead + (width,), F32)
                for width in (SUBLANES - N_GROUPS, LANES - SUBLANES - E_PER_GROUP)]
        return jnp.concatenate([jnp.broadcast_to(group_part, lead + (N_GROUPS,)), pads[0],
                                expert_part, pads[1]], axis=-1)

    w = slabs(w_rg, w_re.reshape(D_MODEL, N_GROUPS, E_PER_GROUP).transpose(1, 0, 2))
    b = slabs(b_rg, b_re.reshape(N_GROUPS, 1, E_PER_GROUP))
    hi = w.astype(BF16)
    lo = (w - hi.astype(F32)).astype(BF16)
    return hi, lo, b


def _layer(x2d, batch, seq_len, norm_mix_g, w_in, q_norm_g, k_norm_g, conv_w, w_sb_branch,
           w_conv_branch, w_out, norm_ffn_g, w_router_group, b_router_group, w_router_expert,
           b_router_expert, w_gate_e, w_up_e, w_down_e):
    n = x2d.shape[0]
    lane = jnp.arange(WIDTH)
    head_mean = jnp.where(lane[:, None] // HEAD_DIM == lane[None, :] // HEAD_DIM,
                          1.0 / HEAD_DIM, 0.0).astype(BF16)
    qg = (jnp.tile(q_norm_g, N_HEADS) * HEAD_DIM ** -0.5)[None, :]
    kg = jnp.tile(k_norm_g, N_HEADS)[None, :]
    qt, k, vt, y, sa, sb = _inproj_call(x2d, norm_mix_g[None, :], w_in.astype(BF16), qg, kg,
                                        head_mean, conv_w, seq_len)

    pos = jnp.arange(TK)
    tri = (pos[None, :] >= pos[:, None]).astype(BF16)
    attn, (wg16, wu16, wd16) = _attn_call(qt, k, vt, tri, (w_gate_e, w_up_e, w_down_e), batch,
                                          seq_len)

    wrh, wrl, br = _router_slabs(w_router_group, b_router_group, w_router_expert, b_router_expert)
    tpos = jnp.arange(TM_PROJ)
    upper = (tpos[:, None] < tpos[None, :]).astype(BF16)
    x1t, route, cnt = _merge_call(attn, y, sa, sb, x2d, w_sb_branch.astype(BF16),
                                  w_conv_branch.astype(BF16), w_out.astype(BF16),
                                  norm_ffn_g[None, :], wrh, wrl, br, upper)

    t = TM_MOE
    grp = route[0].astype(jnp.int32)
    rank = route[1].astype(jnp.int32)
    counts = cnt[0:N_GROUPS, 0].astype(jnp.int32)
    padded = ((counts + t - 1) // t) * t
    ends = jnp.cumsum(padded)
    offs = ends - padded
    dest = rank
    for g in range(N_GROUPS):
        dest = dest + jnp.where(grp == g, offs[g], 0)
    n_tiles = n // t + N_GROUPS
    tile_start = jnp.arange(n_tiles, dtype=jnp.int32) * t
    tile_valid = (tile_start < ends[-1]).astype(jnp.int32)
    tile_group = jnp.zeros_like(tile_start)
    for g in range(N_GROUPS - 1):
        tile_group = tile_group + (jnp.minimum(tile_start, ends[-1] - t) >= ends[g]).astype(jnp.int32)
    p_rows = n_tiles * t
    fill_start = jnp.concatenate([offs + counts, ends[-1:]])
    fill_n = jnp.concatenate([padded - counts, p_rows - ends[-1:]])

    xs = _scatter_call(dest, fill_start, fill_n, x1t, p_rows)
    ys = _moe_call(tile_group, tile_valid, xs, norm_ffn_g[None, :], wrh, wrl, br, wg16, wu16, wd16)
    return _gather_call(dest, ys, n)


def kernel(x, norm_mix_g, w_in, q_norm_g, k_norm_g, conv_w, w_sb_branch, w_conv_branch, w_out,
           norm_ffn_g, w_router_group, b_router_group, w_router_expert, b_router_expert,
           w_gate_e, w_up_e, w_down_e):
    batch, seq_len, d = x.shape
    x2d = x.reshape(batch * seq_len, d)
    for l in range(norm_mix_g.shape[0]):
        x2d = _layer(x2d, batch, seq_len, norm_mix_g[l], w_in[l], q_norm_g[l], k_norm_g[l],
                     conv_w[l], w_sb_branch[l], w_conv_branch[l], w_out[l], norm_ffn_g[l],
                     w_router_group[l], b_router_group[l], w_router_expert[l], b_router_expert[l],
                     w_gate_e[l], w_up_e[l], w_down_e[l])
    return x2d.reshape(batch, seq_len, d)
```

```python
import functools

import jax
import jax.numpy as jnp
from jax import lax
from jax.experimental import pallas as pl
from jax.experimental.pallas import tpu as pltpu

F32 = jnp.float32
BF16 = jnp.bfloat16

D_MODEL = 1024
N_HEADS = 8
HEAD_DIM = 64
WIDTH = N_HEADS * HEAD_DIM
CONV_K = 3
N_GROUPS = 4
E_PER_GROUP = 8
N_EXPERTS = N_GROUPS * E_PER_GROUP
D_EXPERT = 256
EPS = 1e-6

LANES = 128
SUBLANES = 8
HEADS_PER_STEP = LANES // HEAD_DIM
N_PAIRS = N_HEADS // HEADS_PER_STEP

TM_PROJ = 512
TQ = 256
TK = 256
TM_MOE = 512
TM_PERM = 2048
ISSUE_UNROLL = 8
VMEM_LIMIT = 56 * 1024 * 1024

def _dot(a, b):
    return jnp.dot(a, b, preferred_element_type=F32)


def _split_bf16(v):
    hi = v.astype(BF16)
    lo = (v - hi.astype(F32)).astype(BF16)
    return hi, lo


def _inproj_kernel(x_ref, g1_ref, w_ref, qg_ref, kg_ref, pm_ref, cw_ref,
                   qt_ref, k_ref, vt_ref, y_ref, sa_ref, sb_ref, cu_scr, *, tiles_per_seq):
    tm = x_ref.shape[0]

    @pl.when(pl.program_id(0) % tiles_per_seq == 0)
    def _():
        cu_scr[0:SUBLANES, :] = jnp.zeros((SUBLANES, WIDTH), F32)

    x = x_ref[...]
    ms = jnp.mean(x * x, axis=-1, keepdims=True)
    h = (x * lax.rsqrt(ms + EPS) * g1_ref[...]).astype(BF16)

    def proj(lo, width):
        return _dot(h, w_ref[:, lo:lo + width])

    def head_norm(t, g_ref):
        hms = _dot((t * t).astype(BF16), pm_ref[...])
        return t * lax.rsqrt(hms + EPS) * g_ref[...]

    sa_ref[...] = jax.nn.sigmoid(proj(6 * WIDTH, D_MODEL))
    sb_ref[...] = jax.nn.sigmoid(proj(6 * WIDTH + D_MODEL, D_MODEL))

    c_b = proj(3 * WIDTH, WIDTH)
    cu = proj(4 * WIDTH, WIDTH) * proj(5 * WIDTH, WIDTH)
    cu_scr[SUBLANES:SUBLANES + tm, :] = cu
    cw = cw_ref[...]
    conv = (cw[0:1, :] * cu_scr[SUBLANES - 2:SUBLANES - 2 + tm, :]
            + cw[1:2, :] * cu_scr[SUBLANES - 1:SUBLANES - 1 + tm, :]
            + cw[2:3, :] * cu)
    y_ref[...] = (c_b * conv).astype(BF16)
    cu_scr[0:SUBLANES, :] = cu_scr[tm:tm + SUBLANES, :]

    q = head_norm(proj(0, WIDTH), qg_ref)
    k_ref[...] = head_norm(proj(WIDTH, WIDTH), kg_ref).astype(BF16)
    v = proj(2 * WIDTH, WIDTH)
    for blk in range(tm // TQ):
        qt_ref[blk] = q[blk * TQ:(blk + 1) * TQ, :].T.astype(BF16)
        vt_ref[blk] = v[blk * TK:(blk + 1) * TK, :].T.astype(BF16)


def _inproj_call(x2d, g1, w_in, qg, kg, pm, conv_w, seq_len):
    n = x2d.shape[0]
    tm = TM_PROJ
    in_w = w_in.shape[1]
    const = lambda shape: pl.BlockSpec(shape, lambda i: (0, 0), pipeline_mode=pl.Buffered(1))
    rows = lambda width: pl.BlockSpec((tm, width), lambda i: (i, 0))
    assert TQ == TK and tm % TQ == 0
    cols = pl.BlockSpec((tm // TQ, WIDTH, TQ), lambda i: (i, 0, 0))
    cols_shape = jax.ShapeDtypeStruct((n // TQ, WIDTH, TQ), BF16)
    return pl.pallas_call(
        functools.partial(_inproj_kernel, tiles_per_seq=seq_len // tm),
        grid=(n // tm,),
        in_specs=[rows(D_MODEL), const((1, D_MODEL)), const((D_MODEL, in_w)), const((1, WIDTH)),
                  const((1, WIDTH)), const((WIDTH, WIDTH)), const((CONV_K, WIDTH))],
        out_specs=[cols, rows(WIDTH), cols, rows(WIDTH), rows(D_MODEL), rows(D_MODEL)],
        out_shape=[cols_shape, jax.ShapeDtypeStruct((n, WIDTH), BF16)] * 2
                  + [jax.ShapeDtypeStruct((n, D_MODEL), F32)] * 2,
        scratch_shapes=[pltpu.VMEM((SUBLANES + tm, WIDTH), F32)],
        compiler_params=pltpu.CompilerParams(dimension_semantics=("arbitrary",),
                                             vmem_limit_bytes=VMEM_LIMIT),
        name="inproj",
    )(x2d, g1, w_in, qg, kg, pm, conv_w)


def _softplus(z):
    return jnp.maximum(z, 0.0) + jnp.log(1.0 + jnp.exp(-jnp.abs(z)))


def _attn_kernel(qt_ref, k_ref, vt_ref, tri_ref, wg_ref, wu_ref, wd_ref, o_ref, wg16_ref, wu16_ref,
                 wd16_ref, qm_scr, z_scr, zc_scr, suf_scr, a_scr, acc_scr, c_scr):
    i = pl.program_id(1)
    tq = qt_ref.shape[2]
    feat = lax.broadcasted_iota(jnp.int32, (LANES, tq), 0)
    first_head = feat < HEAD_DIM
    for p in range(N_PAIRS):
        q2 = qt_ref[0, p * LANES:(p + 1) * LANES, :]
        zero = jnp.zeros_like(q2)
        qm_scr[2 * p] = jnp.where(first_head, q2, zero)
        qm_scr[2 * p + 1] = jnp.where(first_head, zero, q2)
    acc_scr[...] = jnp.zeros_like(acc_scr)
    c_scr[...] = jnp.zeros_like(c_scr)

    def pair(s):
        p = s // HEADS_PER_STEP
        return slice(p * LANES, (p + 1) * LANES)

    def scores(s, j):
        keys = k_ref[pl.ds(pl.multiple_of(j * TK, TK), TK), pair(s)]
        z_scr[s] = _dot(keys, qm_scr[s])

    def suffix_sums(s, mask):
        z = z_scr[s]
        zc_scr[s] = z - c_scr[s]
        sp = _softplus(z.astype(BF16))
        if mask is not None:
            sp = jnp.where(mask, sp, jnp.zeros_like(sp))
        suf_scr[s] = _dot(tri_ref[...], sp)

    def weights(s, mask):
        suffix = suf_scr[s]
        a = jnp.exp(zc_scr[s] - suffix)
        if mask is not None:
            a = jnp.where(mask, a, 0.0)
        a_scr[s] = a.astype(BF16)
        c_scr[s] = c_scr[s] + suffix[0:1, :]

    def accumulate(s, j):
        acc_scr[s] = acc_scr[s] + _dot(vt_ref[j, pair(s), :], a_scr[s])

    key = lax.broadcasted_iota(jnp.int32, (TK, tq), 0)
    qry = lax.broadcasted_iota(jnp.int32, (TK, tq), 1)
    causal = key < qry
    nxt = jnp.maximum(i - 1, 0)
    for s in range(N_HEADS):
        scores(s, i)
    for s in range(N_HEADS):
        suffix_sums(s, causal)
        scores(s, nxt)
    for s in range(N_HEADS):
        weights(s, causal)

    def body(t, carry):
        j = i - t
        nxt = jnp.maximum(j - 1, 0)
        for s in range(N_HEADS):
            accumulate(s, j + 1)
            suffix_sums(s, None)
            scores(s, nxt)
        for s in range(N_HEADS):
            weights(s, None)
        return carry

    lax.fori_loop(1, i + 1, body, 0)
    for s in range(N_HEADS):
        accumulate(s, 0)
    for p in range(N_PAIRS):
        both = jnp.where(first_head, acc_scr[2 * p], acc_scr[2 * p + 1])
        o_ref[:, p * LANES:(p + 1) * LANES] = both.T.astype(o_ref.dtype)

    wg16_ref[...] = wg_ref[...].astype(BF16)
    wu16_ref[...] = wu_ref[...].astype(BF16)
    wd16_ref[...] = wd_ref[...].astype(BF16)


def _attn_call(qt, k, vt, tri, expert_weights, batch, seq_len):
    assert TQ == TK
    nq = seq_len // TQ
    steps = batch * nq
    qspec = pl.BlockSpec((1, WIDTH, TQ), lambda b, i: (b * nq + i, 0, 0))
    kspec = pl.BlockSpec((seq_len, WIDTH), lambda b, i: (b, 0))
    vspec = pl.BlockSpec((seq_len // TK, WIDTH, TK), lambda b, i: (b, 0, 0))
    ospec = pl.BlockSpec((TQ, WIDTH), lambda b, i: (b * nq + i, 0))
    slabs = [w.reshape(steps, -1, w.shape[-1]) for w in expert_weights]
    wspecs = [pl.BlockSpec((1,) + s.shape[1:], lambda b, i: (b * nq + i, 0, 0)) for s in slabs]
    attn, *cast = pl.pallas_call(
        _attn_kernel,
        grid=(batch, nq),
        in_specs=[qspec, kspec, vspec, pl.BlockSpec((TK, TK), lambda b, i: (0, 0))] + wspecs,
        out_specs=[ospec] + wspecs,
        out_shape=[jax.ShapeDtypeStruct(k.shape, BF16)]
                  + [jax.ShapeDtypeStruct(s.shape, BF16) for s in slabs],
        scratch_shapes=[pltpu.VMEM((N_HEADS, LANES, TQ), BF16),
                        pltpu.VMEM((N_HEADS, TK, TQ), F32),
                        pltpu.VMEM((N_HEADS, TK, TQ), F32),
                        pltpu.VMEM((N_HEADS, TK, TQ), F32),
                        pltpu.VMEM((N_HEADS, TK, TQ), BF16),
                        pltpu.VMEM((N_HEADS, LANES, TQ), F32),
                        pltpu.VMEM((N_HEADS, 1, TQ), F32)],
        compiler_params=pltpu.CompilerParams(dimension_semantics=("arbitrary", "arbitrary"),
                                             vmem_limit_bytes=VMEM_LIMIT),
        name="sb_attn",
    )(qt, k, vt, tri, *slabs)
    return attn, [c.reshape(w.shape) for c, w in zip(cast, expert_weights)]


def _router_logits(w_hi_ref, w_lo_ref, b_ref, h):
    hh, hl = _split_bf16(h)
    logits = _dot(hh, w_hi_ref[0]) + _dot(hl, w_hi_ref[0]) + _dot(hh, w_lo_ref[0]) + b_ref[0]
    return logits.T


def _first_argmax(vals, rowi):
    top = jnp.max(vals, axis=0, keepdims=True)
    idx = jnp.min(jnp.where(vals == top, rowi, jnp.float32(SUBLANES)), axis=0, keepdims=True)
    return top, idx


def _ffn_norm(x1, g2_ref):
    ms = jnp.mean(x1 * x1, axis=-1, keepdims=True)
    return x1 * lax.rsqrt(ms + EPS) * g2_ref[...]


def _merge_kernel(attn_ref, y_ref, sa_ref, sb_ref, x_ref, wsb_ref, wcv_ref, wout_ref, g2_ref,
                  wrh_ref, wrl_ref, br_ref, upper_ref, x1t_ref, route_ref, cnt_ref, cnt_scr):
    tm = x_ref.shape[0]

    @pl.when(pl.program_id(0) == 0)
    def _():
        cnt_scr[...] = jnp.zeros_like(cnt_scr)

    branch_a = _dot(attn_ref[...], wsb_ref[...])
    branch_b = _dot(y_ref[...], wcv_ref[...])
    merged = (sa_ref[...] * branch_a + sb_ref[...] * branch_b).astype(BF16)
    x1 = x_ref[...] + _dot(merged, wout_ref[...])
    x1t_ref[...] = x1.reshape(tm, SUBLANES, LANES)

    logits = _router_logits(wrh_ref, wrl_ref, br_ref, _ffn_norm(x1, g2_ref))
    rowi = lax.broadcasted_iota(jnp.int32, (SUBLANES, tm), 0).astype(F32)
    gl = jnp.where(rowi < N_GROUPS, logits[0:SUBLANES, :], jnp.float32(-jnp.inf))
    _, gidx = _first_argmax(gl, rowi)

    onehot = jnp.where(rowi == gidx, 1.0, 0.0)
    before = _dot(onehot.astype(BF16), upper_ref[...]) + cnt_scr[:, 0:1]
    rank = jnp.sum(onehot * before, axis=0, keepdims=True)
    cnt_scr[...] = cnt_scr[...] + jnp.sum(onehot, axis=1, keepdims=True)
    cnt_ref[...] = cnt_scr[...]
    route_ref[...] = jnp.where(rowi == 0, gidx, jnp.where(rowi == 1, rank, 0.0))


def _merge_call(attn, y, sa, sb, x2d, wsb, wcv, wout, g2, wrh, wrl, br, upper):
    n = x2d.shape[0]
    tm = TM_PROJ
    rows = lambda width: pl.BlockSpec((tm, width), lambda i: (i, 0))
    const = lambda shape: pl.BlockSpec(shape, lambda i: (0,) * len(shape),
                                       pipeline_mode=pl.Buffered(1))
    slab = (1, D_MODEL, LANES)
    return pl.pallas_call(
        _merge_kernel,
        grid=(n // tm,),
        in_specs=[rows(WIDTH), rows(WIDTH), rows(D_MODEL), rows(D_MODEL), rows(D_MODEL),
                  const((WIDTH, D_MODEL)), const((WIDTH, D_MODEL)), const((D_MODEL, D_MODEL)),
                  const((1, D_MODEL)), const(slab), const(slab), const((1, 1, LANES)),
                  const((tm, tm))],
        out_specs=[pl.BlockSpec((tm, SUBLANES, LANES), lambda i: (i, 0, 0)),
                   pl.BlockSpec((SUBLANES, tm), lambda i: (0, i)),
                   pl.BlockSpec((SUBLANES, LANES), lambda i: (0, 0))],
        out_shape=[jax.ShapeDtypeStruct((n, SUBLANES, LANES), F32),
                   jax.ShapeDtypeStruct((SUBLANES, n), F32),
                   jax.ShapeDtypeStruct((SUBLANES, LANES), F32)],
        scratch_shapes=[pltpu.VMEM((SUBLANES, LANES), F32)],
        compiler_params=pltpu.CompilerParams(dimension_semantics=("arbitrary",),
                                             vmem_limit_bytes=VMEM_LIMIT),
        name="merge_router",
    )(attn, y, sa, sb, x2d, wsb, wcv, wout, g2, wrh, wrl, br, upper)


def _row_copy(src_ref, src_row, dst_ref, dst_row, sem):
    return pltpu.make_async_copy(src_ref.at[src_row], dst_ref.at[dst_row], sem)


def _scatter_kernel(dest_ref, fill_start_ref, fill_n_ref, x_ref, xs_ref, zero_scr, sem):
    i = pl.program_id(0)
    tm = x_ref.shape[0]
    base = i * tm

    def start(k, carry):
        for u in range(ISSUE_UNROLL):
            r = k * ISSUE_UNROLL + u
            _row_copy(x_ref, r, xs_ref, dest_ref[base + r], sem).start(priority=u % 2)
        return carry

    lax.fori_loop(0, tm // ISSUE_UNROLL, start, 0)
    pltpu.make_async_copy(x_ref, xs_ref.at[pl.ds(0, tm)], sem).wait()

    @pl.when(i == pl.num_programs(0) - 1)
    def _():
        zero_scr[...] = jnp.zeros_like(zero_scr)
        for g in range(N_GROUPS + 1):
            first = fill_start_ref[g]

            def start_fill(r, carry):
                _row_copy(zero_scr, 0, xs_ref, first + r, sem).start()
                return carry

            def wait_fill(r, carry):
                _row_copy(zero_scr, 0, xs_ref, 0, sem).wait()
                return carry

            lax.fori_loop(0, fill_n_ref[g], start_fill, 0)
            lax.fori_loop(0, fill_n_ref[g], wait_fill, 0)


def _scatter_call(dest, fill_start, fill_n, x1t, p_rows):
    n = x1t.shape[0]
    tm = TM_PERM
    grid_spec = pltpu.PrefetchScalarGridSpec(
        num_scalar_prefetch=3,
        grid=(n // tm,),
        in_specs=[pl.BlockSpec((tm, SUBLANES, LANES), lambda i, *_: (i, 0, 0))],
        out_specs=pl.BlockSpec(memory_space=pl.ANY),
        scratch_shapes=[pltpu.VMEM((1, SUBLANES, LANES), F32), pltpu.SemaphoreType.DMA(())],
    )
    return pl.pallas_call(
        _scatter_kernel,
        grid_spec=grid_spec,
        out_shape=jax.ShapeDtypeStruct((p_rows, SUBLANES, LANES), F32),
        compiler_params=pltpu.CompilerParams(dimension_semantics=("arbitrary",),
                                             vmem_limit_bytes=VMEM_LIMIT),
        name="row_scatter",
    )(dest, fill_start, fill_n, x1t)


def _moe_kernel(tg_ref, tv_ref, xs_ref, g2_ref, wrh_ref, wrl_ref, br_ref, wg_ref, wu_ref, wd_ref,
                o_ref):
    i = pl.program_id(0)
    t = xs_ref.shape[0]

    @pl.when(tv_ref[i] == 0)
    def _():
        o_ref[...] = jnp.zeros_like(o_ref)

    @pl.when(tv_ref[i] > 0)
    def _():
        x1 = xs_ref[...].reshape(t, D_MODEL)
        h2 = _ffn_norm(x1, g2_ref)

        rowi = lax.broadcasted_iota(jnp.int32, (SUBLANES, t), 0).astype(F32)
        neg = jnp.float32(-jnp.inf)
        logits = _router_logits(wrh_ref, wrl_ref, br_ref, h2)
        gl = jnp.where(rowi < N_GROUPS, logits[0:SUBLANES, :], neg)
        mine = jnp.sum(jnp.where(rowi == tg_ref[i].astype(F32), gl, 0.0), axis=0, keepdims=True)
        g_w = 1.0 / jnp.sum(jnp.exp(gl - mine), axis=0, keepdims=True)
        el = logits[SUBLANES:2 * SUBLANES, :]
        t1, i1 = _first_argmax(el, rowi)
        t2, i2 = _first_argmax(jnp.where(rowi == i1, neg, el), rowi)
        e = jnp.exp(t2 - t1)
        w1 = g_w / (1.0 + e)
        w2 = g_w * e / (1.0 + e)
        cw_t = jnp.where(rowi == i1, w1, 0.0) + jnp.where(rowi == i2, w2, 0.0)
        cw = jnp.concatenate([cw_t, jnp.zeros((LANES - SUBLANES, t), F32)], axis=0).T

        hb = h2.astype(BF16)
        gate_up = lambda ex: (_dot(hb, wg_ref[ex]), _dot(hb, wu_ref[ex]))
        acc = x1
        ahead = gate_up(0)
        for ex in range(E_PER_GROUP):
            gate, up = ahead
            if ex + 1 < E_PER_GROUP:
                ahead = gate_up(ex + 1)
            act = (gate * jax.nn.sigmoid(gate)) * up * cw[:, ex:ex + 1]
            acc = acc + _dot(act.astype(BF16), wd_ref[ex])
        o_ref[...] = acc.reshape(t, SUBLANES, LANES)


def _moe_call(tile_group, tile_valid, xs, g2, wrh, wrl, br, wg, wu, wd):
    p = xs.shape[0]
    t = TM_MOE
    by_block = lambda i, tg, tv: (i, 0, 0)
    by_group = lambda i, tg, tv: (tg[i], 0, 0)
    once = pl.Buffered(1)
    grid_spec = pltpu.PrefetchScalarGridSpec(
        num_scalar_prefetch=2,
        grid=(p // t,),
        in_specs=[pl.BlockSpec((t, SUBLANES, LANES), by_block),
                  pl.BlockSpec((1, D_MODEL), lambda i, *_: (0, 0)),
                  pl.BlockSpec((1, D_MODEL, LANES), by_group, pipeline_mode=once),
                  pl.BlockSpec((1, D_MODEL, LANES), by_group, pipeline_mode=once),
                  pl.BlockSpec((1, 1, LANES), by_group, pipeline_mode=once),
                  pl.BlockSpec((E_PER_GROUP, D_MODEL, D_EXPERT), by_group, pipeline_mode=once),
                  pl.BlockSpec((E_PER_GROUP, D_MODEL, D_EXPERT), by_group, pipeline_mode=once),
                  pl.BlockSpec((E_PER_GROUP, D_EXPERT, D_MODEL), by_group, pipeline_mode=once)],
        out_specs=pl.BlockSpec((t, SUBLANES, LANES), by_block),
    )
    return pl.pallas_call(
        _moe_kernel,
        grid_spec=grid_spec,
        out_shape=jax.ShapeDtypeStruct((p, SUBLANES, LANES), F32),
        compiler_params=pltpu.CompilerParams(dimension_semantics=("arbitrary",),
                                             vmem_limit_bytes=VMEM_LIMIT),
        name="moe_experts",
    )(tile_group, tile_valid, xs, g2, wrh, wrl, br, wg, wu, wd)


def _gather_kernel(dest_ref, ys_ref, o_ref, rows_scr, sem):
    i = pl.program_id(0)
    tm = o_ref.shape[0]
    base = i * tm

    def start(k, carry):
        for u in range(ISSUE_UNROLL):
            r = k * ISSUE_UNROLL + u
            _row_copy(ys_ref, dest_ref[base + r], rows_scr, r, sem).start(priority=u % 2)
        return carry

    lax.fori_loop(0, tm // ISSUE_UNROLL, start, 0)
    pltpu.make_async_copy(ys_ref.at[pl.ds(0, tm)], rows_scr, sem).wait()
    o_ref[...] = rows_scr[...].reshape(tm, D_MODEL)


def _gather_call(dest, ys, n):
    tm = TM_PERM
    grid_spec = pltpu.PrefetchScalarGridSpec(
        num_scalar_prefetch=1,
        grid=(n // tm,),
        in_specs=[pl.BlockSpec(memory_space=pl.ANY)],
        out_specs=pl.BlockSpec((tm, D_MODEL), lambda i, *_: (i, 0)),
        scratch_shapes=[pltpu.VMEM((tm, SUBLANES, LANES), F32), pltpu.SemaphoreType.DMA(())],
    )
    return pl.pallas_call(
        _gather_kernel,
        grid_spec=grid_spec,
        out_shape=jax.ShapeDtypeStruct((n, D_MODEL), F32),
        compiler_params=pltpu.CompilerParams(dimension_semantics=("arbitrary",),
                                             vmem_limit_bytes=VMEM_LIMIT),
        name="row_gather",
    )(dest, ys)


def _router_slabs(w_rg, b_rg, w_re, b_re):
    def slabs(group_part, expert_part):
        lead = expert_part.shape[:-1]
        pads = [jnp.zeros(lead + (width,), F32)
                for width in (SUBLANES - N_GROUPS, LANES - SUBLANES - E_PER_GROUP)]
        return jnp.concatenate([jnp.broadcast_to(group_part, lead + (N_GROUPS,)), pads[0],
                                expert_part, pads[1]], axis=-1)

    w = slabs(w_rg, w_re.reshape(D_MODEL, N_GROUPS, E_PER_GROUP).transpose(1, 0, 2))
    b = slabs(b_rg, b_re.reshape(N_GROUPS, 1, E_PER_GROUP))
    hi = w.astype(BF16)
    lo = (w - hi.astype(F32)).astype(BF16)
    return hi, lo, b


def _layer(x2d, batch, seq_len, norm_mix_g, w_in, q_norm_g, k_norm_g, conv_w, w_sb_branch,
           w_conv_branch, w_out, norm_ffn_g, w_router_group, b_router_group, w_router_expert,
           b_router_expert, w_gate_e, w_up_e, w_down_e):
    n = x2d.shape[0]
    lane = jnp.arange(WIDTH)
    head_mean = jnp.where(lane[:, None] // HEAD_DIM == lane[None, :] // HEAD_DIM,
                          1.0 / HEAD_DIM, 0.0).astype(BF16)
    qg = (jnp.tile(q_norm_g, N_HEADS) * HEAD_DIM ** -0.5)[None, :]
    kg = jnp.tile(k_norm_g, N_HEADS)[None, :]
    qt, k, vt, y, sa, sb = _inproj_call(x2d, norm_mix_g[None, :], w_in.astype(BF16), qg, kg,
                                        head_mean, conv_w, seq_len)

    pos = jnp.arange(TK)
    tri = (pos[None, :] >= pos[:, None]).astype(BF16)
    attn, (wg16, wu16, wd16) = _attn_call(qt, k, vt, tri, (w_gate_e, w_up_e, w_down_e), batch,
                                          seq_len)

    wrh, wrl, br = _router_slabs(w_router_group, b_router_group, w_router_expert, b_router_expert)
    tpos = jnp.arange(TM_PROJ)
    upper = (tpos[:, None] < tpos[None, :]).astype(BF16)
    x1t, route, cnt = _merge_call(attn, y, sa, sb, x2d, w_sb_branch.astype(BF16),
                                  w_conv_branch.astype(BF16), w_out.astype(BF16),
                                  norm_ffn_g[None, :], wrh, wrl, br, upper)

    t = TM_MOE
    grp = route[0].astype(jnp.int32)
    rank = route[1].astype(jnp.int32)
    counts = cnt[0:N_GROUPS, 0].astype(jnp.int32)
    padded = ((counts + t - 1) // t) * t
    ends = jnp.cumsum(padded)
    offs = ends - padded
    dest = rank
    for g in range(N_GROUPS):
        dest = dest + jnp.where(grp == g, offs[g], 0)
    n_tiles = n // t + N_GROUPS
    tile_start = jnp.arange(n_tiles, dtype=jnp.int32) * t
    tile_valid = (tile_start < ends[-1]).astype(jnp.int32)
    tile_group = jnp.zeros_like(tile_start)
    for g in range(N_GROUPS - 1):
        tile_group = tile_group + (jnp.minimum(tile_start, ends[-1] - t) >= ends[g]).astype(jnp.int32)
    p_rows = n_tiles * t
    fill_start = jnp.concatenate([offs + counts, ends[-1:]])
    fill_n = jnp.concatenate([padded - counts, p_rows - ends[-1:]])

    xs = _scatter_call(dest, fill_start, fill_n, x1t, p_rows)
    ys = _moe_call(tile_group, tile_valid, xs, norm_ffn_g[None, :], wrh, wrl, br, wg16, wu16, wd16)
    return _gather_call(dest, ys, n)


def kernel(x, norm_mix_g, w_in, q_norm_g, k_norm_g, conv_w, w_sb_branch, w_conv_branch, w_out,
           norm_ffn_g, w_router_group, b_router_group, w_router_expert, b_router_expert,
           w_gate_e, w_up_e, w_down_e):
    batch, seq_len, d = x.shape
    x2d = x.reshape(batch * seq_len, d)
    for l in range(norm_mix_g.shape[0]):
        x2d = _layer(x2d, batch, seq_len, norm_mix_g[l], w_in[l], q_norm_g[l], k_norm_g[l],
                     conv_w[l], w_sb_branch[l], w_conv_branch[l], w_out[l], norm_ffn_g[l],
                     w_router_group[l], b_router_group[l], w_router_expert[l], b_router_expert[l],
                     w_gate_e[l], w_up_e[l], w_down_e[l])
    return x2d.reshape(batch, seq_len, d)
```

```python
import functools

import jax
import jax.numpy as jnp
from jax import lax
from jax.experimental import pallas as pl
from jax.experimental.pallas import tpu as pltpu

F32 = jnp.float32
BF16 = jnp.bfloat16

D_MODEL = 1024
N_HEADS = 8
HEAD_DIM = 64
WIDTH = N_HEADS * HEAD_DIM
CONV_K = 3
N_GROUPS = 4
E_PER_GROUP = 8
N_EXPERTS = N_GROUPS * E_PER_GROUP
D_EXPERT = 256
EPS = 1e-6

LANES = 128
SUBLANES = 8
HEADS_PER_STEP = LANES // HEAD_DIM
N_PAIRS = N_HEADS // HEADS_PER_STEP

TM_PROJ = 512
TQ = 256
TK = 256
MERGE_BAND = 256
TM_MOE = 512
TM_PERM = 4096
ISSUE_UNROLL = 8
VMEM_LIMIT = 56 * 1024 * 1024

def _dot(a, b):
    return jnp.dot(a, b, preferred_element_type=F32)


def _split_bf16(v):
    hi = v.astype(BF16)
    lo = (v - hi.astype(F32)).astype(BF16)
    return hi, lo


def _inproj_kernel(x_ref, g1_ref, w_ref, qg_ref, kg_ref, pm_ref, cw_ref,
                   qt_ref, k_ref, vt_ref, y_ref, sa_ref, sb_ref, cu_scr, *, tiles_per_seq):
    tm = x_ref.shape[0]

    @pl.when(pl.program_id(0) % tiles_per_seq == 0)
    def _():
        cu_scr[0:SUBLANES, :] = jnp.zeros((SUBLANES, WIDTH), F32)

    x = x_ref[...]
    ms = jnp.mean(x * x, axis=-1, keepdims=True)
    h = (x * lax.rsqrt(ms + EPS) * g1_ref[...]).astype(BF16)

    def proj(lo, width):
        return _dot(h, w_ref[:, lo:lo + width])

    def head_norm(t, g_ref):
        hms = _dot((t * t).astype(BF16), pm_ref[...])
        return t * lax.rsqrt(hms + EPS) * g_ref[...]

    sa_ref[...] = jax.nn.sigmoid(proj(6 * WIDTH, D_MODEL))
    sb_ref[...] = jax.nn.sigmoid(proj(6 * WIDTH + D_MODEL, D_MODEL))

    c_b = proj(3 * WIDTH, WIDTH)
    cu = proj(4 * WIDTH, WIDTH) * proj(5 * WIDTH, WIDTH)
    cu_scr[SUBLANES:SUBLANES + tm, :] = cu
    cw = cw_ref[...]
    conv = (cw[0:1, :] * cu_scr[SUBLANES - 2:SUBLANES - 2 + tm, :]
            + cw[1:2, :] * cu_scr[SUBLANES - 1:SUBLANES - 1 + tm, :]
            + cw[2:3, :] * cu)
    y_ref[...] = (c_b * conv).astype(BF16)
    cu_scr[0:SUBLANES, :] = cu_scr[tm:tm + SUBLANES, :]

    q = head_norm(proj(0, WIDTH), qg_ref)
    k_ref[...] = head_norm(proj(WIDTH, WIDTH), kg_ref).astype(BF16)
    v = proj(2 * WIDTH, WIDTH)
    for blk in range(tm // TQ):
        qt_ref[blk] = q[blk * TQ:(blk + 1) * TQ, :].T.astype(BF16)
        vt_ref[blk] = v[blk * TK:(blk + 1) * TK, :].T.astype(BF16)


def _inproj_call(x2d, g1, w_in, qg, kg, pm, conv_w, seq_len):
    n = x2d.shape[0]
    tm = TM_PROJ
    in_w = w_in.shape[1]
    const = lambda shape: pl.BlockSpec(shape, lambda i: (0, 0), pipeline_mode=pl.Buffered(1))
    rows = lambda width: pl.BlockSpec((tm, width), lambda i: (i, 0))
    assert TQ == TK and tm % TQ == 0
    cols = pl.BlockSpec((tm // TQ, WIDTH, TQ), lambda i: (i, 0, 0))
    cols_shape = jax.ShapeDtypeStruct((n // TQ, WIDTH, TQ), BF16)
    return pl.pallas_call(
        functools.partial(_inproj_kernel, tiles_per_seq=seq_len // tm),
        grid=(n // tm,),
        in_specs=[rows(D_MODEL), const((1, D_MODEL)), const((D_MODEL, in_w)), const((1, WIDTH)),
                  const((1, WIDTH)), const((WIDTH, WIDTH)), const((CONV_K, WIDTH))],
        out_specs=[cols, rows(WIDTH), cols, rows(WIDTH), rows(D_MODEL), rows(D_MODEL)],
        out_shape=[cols_shape, jax.ShapeDtypeStruct((n, WIDTH), BF16)] * 2
                  + [jax.ShapeDtypeStruct((n, D_MODEL), F32)] * 2,
        scratch_shapes=[pltpu.VMEM((SUBLANES + tm, WIDTH), F32)],
        compiler_params=pltpu.CompilerParams(dimension_semantics=("arbitrary",),
                                             vmem_limit_bytes=VMEM_LIMIT),
        name="inproj",
    )(x2d, g1, w_in, qg, kg, pm, conv_w)


def _softplus(z):
    return jnp.maximum(z, 0.0) + jnp.log(1.0 + jnp.exp(-jnp.abs(z)))


def _attn_kernel(qt_ref, k_ref, vt_ref, tri_ref, wg_ref, wu_ref, wd_ref, o_ref, wg16_ref, wu16_ref,
                 wd16_ref, qm_scr, z_scr, zc_scr, suf_scr, a_scr, acc_scr, c_scr):
    i = pl.program_id(1)
    tq = qt_ref.shape[2]
    feat = lax.broadcasted_iota(jnp.int32, (LANES, tq), 0)
    first_head = feat < HEAD_DIM
    for p in range(N_PAIRS):
        q2 = qt_ref[0, p * LANES:(p + 1) * LANES, :]
        zero = jnp.zeros_like(q2)
        qm_scr[2 * p] = jnp.where(first_head, q2, zero)
        qm_scr[2 * p + 1] = jnp.where(first_head, zero, q2)
    acc_scr[...] = jnp.zeros_like(acc_scr)
    c_scr[...] = jnp.zeros_like(c_scr)

    def pair(s):
        p = s // HEADS_PER_STEP
        return slice(p * LANES, (p + 1) * LANES)

    def scores(s, j):
        keys = k_ref[pl.ds(pl.multiple_of(j * TK, TK), TK), pair(s)]
        z_scr[s] = _dot(keys, qm_scr[s])

    def suffix_sums(s, mask):
        z = z_scr[s]
        zc_scr[s] = z - c_scr[s]
        sp = _softplus(z.astype(BF16))
        if mask is not None:
            sp = jnp.where(mask, sp, jnp.zeros_like(sp))
        suf_scr[s] = _dot(tri_ref[...], sp)

    def weights(s, mask):
        suffix = suf_scr[s]
        a = jnp.exp(zc_scr[s] - suffix)
        if mask is not None:
            a = jnp.where(mask, a, 0.0)
        a_scr[s] = a.astype(BF16)
        c_scr[s] = c_scr[s] + suffix[0:1, :]

    def accumulate(s, j):
        acc_scr[s] = acc_scr[s] + _dot(vt_ref[j, pair(s), :], a_scr[s])

    key = lax.broadcasted_iota(jnp.int32, (TK, tq), 0)
    qry = lax.broadcasted_iota(jnp.int32, (TK, tq), 1)
    causal = key < qry
    nxt = jnp.maximum(i - 1, 0)
    for s in range(N_HEADS):
        scores(s, i)
    for s in range(N_HEADS):
        suffix_sums(s, causal)
        scores(s, nxt)
    for s in range(N_HEADS):
        weights(s, causal)

    def body(t, carry):
        j = i - t
        nxt = jnp.maximum(j - 1, 0)
        for s in range(N_HEADS):
            accumulate(s, j + 1)
            suffix_sums(s, None)
            scores(s, nxt)
        for s in range(N_HEADS):
            weights(s, None)
        return carry

    lax.fori_loop(1, i + 1, body, 0)
    for s in range(N_HEADS):
        accumulate(s, 0)
    for p in range(N_PAIRS):
        both = jnp.where(first_head, acc_scr[2 * p], acc_scr[2 * p + 1])
        o_ref[:, p * LANES:(p + 1) * LANES] = both.T.astype(o_ref.dtype)

    wg16_ref[...] = wg_ref[...].astype(BF16)
    wu16_ref[...] = wu_ref[...].astype(BF16)
    wd16_ref[...] = wd_ref[...].astype(BF16)


def _attn_call(qt, k, vt, tri, expert_weights, batch, seq_len):
    assert TQ == TK
    nq = seq_len // TQ
    steps = batch * nq
    qspec = pl.BlockSpec((1, WIDTH, TQ), lambda b, i: (b * nq + i, 0, 0))
    kspec = pl.BlockSpec((seq_len, WIDTH), lambda b, i: (b, 0))
    vspec = pl.BlockSpec((seq_len // TK, WIDTH, TK), lambda b, i: (b, 0, 0))
    ospec = pl.BlockSpec((TQ, WIDTH), lambda b, i: (b * nq + i, 0))
    slabs = [w.reshape(steps, -1, w.shape[-1]) for w in expert_weights]
    wspecs = [pl.BlockSpec((1,) + s.shape[1:], lambda b, i: (b * nq + i, 0, 0)) for s in slabs]
    attn, *cast = pl.pallas_call(
        _attn_kernel,
        grid=(batch, nq),
        in_specs=[qspec, kspec, vspec, pl.BlockSpec((TK, TK), lambda b, i: (0, 0))] + wspecs,
        out_specs=[ospec] + wspecs,
        out_shape=[jax.ShapeDtypeStruct(k.shape, BF16)]
                  + [jax.ShapeDtypeStruct(s.shape, BF16) for s in slabs],
        scratch_shapes=[pltpu.VMEM((N_HEADS, LANES, TQ), BF16),
                        pltpu.VMEM((N_HEADS, TK, TQ), F32),
                        pltpu.VMEM((N_HEADS, TK, TQ), F32),
                        pltpu.VMEM((N_HEADS, TK, TQ), F32),
                        pltpu.VMEM((N_HEADS, TK, TQ), BF16),
                        pltpu.VMEM((N_HEADS, LANES, TQ), F32),
                        pltpu.VMEM((N_HEADS, 1, TQ), F32)],
        compiler_params=pltpu.CompilerParams(dimension_semantics=("arbitrary", "arbitrary"),
                                             vmem_limit_bytes=VMEM_LIMIT),
        name="sb_attn",
    )(qt, k, vt, tri, *slabs)
    return attn, [c.reshape(w.shape) for c, w in zip(cast, expert_weights)]


def _router_logits(w_hi_ref, w_lo_ref, b_ref, h):
    hh, hl = _split_bf16(h)
    logits = _dot(hh, w_hi_ref[0]) + _dot(hl, w_hi_ref[0]) + _dot(hh, w_lo_ref[0]) + b_ref[0]
    return logits.T


def _first_argmax(vals, rowi):
    top = jnp.max(vals, axis=0, keepdims=True)
    idx = jnp.min(jnp.where(vals == top, rowi, jnp.float32(SUBLANES)), axis=0, keepdims=True)
    return top, idx


def _ffn_norm(x1, g2_ref):
    ms = jnp.mean(x1 * x1, axis=-1, keepdims=True)
    return x1 * lax.rsqrt(ms + EPS) * g2_ref[...]


def _merge_kernel(attn_ref, y_ref, sa_ref, sb_ref, x_ref, wsb_ref, wcv_ref, wout_ref, g2_ref,
                  wrh_ref, wrl_ref, br_ref, upper_ref, x1t_ref, route_ref, cnt_ref, cnt_scr):
    tm = x_ref.shape[0]

    @pl.when(pl.program_id(0) == 0)
    def _():
        cnt_scr[...] = jnp.zeros_like(cnt_scr)

    bands = [pl.ds(b * MERGE_BAND, MERGE_BAND) for b in range(tm // MERGE_BAND)]
    merged = [(sa_ref[rows, :] * _dot(attn_ref[rows, :], wsb_ref[...])
               + sb_ref[rows, :] * _dot(y_ref[rows, :], wcv_ref[...])).astype(BF16)
              for rows in bands]
    x1 = [x_ref[rows, :] + _dot(m, wout_ref[...]) for rows, m in zip(bands, merged)]
    logits = []
    for rows, x1_band in zip(bands, x1):
        x1t_ref[rows] = x1_band.reshape(MERGE_BAND, SUBLANES, LANES)
        logits.append(_router_logits(wrh_ref, wrl_ref, br_ref, _ffn_norm(x1_band, g2_ref)))

    rowi = lax.broadcasted_iota(jnp.int32, (SUBLANES, tm), 0).astype(F32)
    gl = jnp.concatenate([lg[0:SUBLANES, :] for lg in logits], axis=1)
    gl = jnp.where(rowi < N_GROUPS, gl, jnp.float32(-jnp.inf))
    _, gidx = _first_argmax(gl, rowi)

    onehot = jnp.where(rowi == gidx, 1.0, 0.0)
    before = _dot(onehot.astype(BF16), upper_ref[...]) + cnt_scr[:, 0:1]
    rank = jnp.sum(onehot * before, axis=0, keepdims=True)
    cnt_scr[...] = cnt_scr[...] + jnp.sum(onehot, axis=1, keepdims=True)
    cnt_ref[...] = cnt_scr[...]
    route_ref[...] = jnp.where(rowi == 0, gidx, jnp.where(rowi == 1, rank, 0.0))


def _merge_call(attn, y, sa, sb, x2d, wsb, wcv, wout, g2, wrh, wrl, br, upper):
    n = x2d.shape[0]
    tm = TM_PROJ
    rows = lambda width: pl.BlockSpec((tm, width), lambda i: (i, 0))
    const = lambda shape: pl.BlockSpec(shape, lambda i: (0,) * len(shape),
                                       pipeline_mode=pl.Buffered(1))
    slab = (1, D_MODEL, LANES)
    return pl.pallas_call(
        _merge_kernel,
        grid=(n // tm,),
        in_specs=[rows(WIDTH), rows(WIDTH), rows(D_MODEL), rows(D_MODEL), rows(D_MODEL),
                  const((WIDTH, D_MODEL)), const((WIDTH, D_MODEL)), const((D_MODEL, D_MODEL)),
                  const((1, D_MODEL)), const(slab), const(slab), const((1, 1, LANES)),
                  const((tm, tm))],
        out_specs=[pl.BlockSpec((tm, SUBLANES, LANES), lambda i: (i, 0, 0)),
                   pl.BlockSpec((SUBLANES, tm), lambda i: (0, i)),
                   pl.BlockSpec((SUBLANES, LANES), lambda i: (0, 0))],
        out_shape=[jax.ShapeDtypeStruct((n, SUBLANES, LANES), F32),
                   jax.ShapeDtypeStruct((SUBLANES, n), F32),
                   jax.ShapeDtypeStruct((SUBLANES, LANES), F32)],
        scratch_shapes=[pltpu.VMEM((SUBLANES, LANES), F32)],
        compiler_params=pltpu.CompilerParams(dimension_semantics=("arbitrary",),
                                             vmem_limit_bytes=VMEM_LIMIT),
        name="merge_router",
    )(attn, y, sa, sb, x2d, wsb, wcv, wout, g2, wrh, wrl, br, upper)


def _row_copy(src_ref, src_row, dst_ref, dst_row, sem):
    return pltpu.make_async_copy(src_ref.at[src_row], dst_ref.at[dst_row], sem)


def _scatter_kernel(dest_ref, fill_start_ref, fill_n_ref, x_ref, xs_ref, zero_scr, sem):
    i = pl.program_id(0)
    tm = x_ref.shape[0]
    base = i * tm

    def start(k, carry):
        for u in range(ISSUE_UNROLL):
            r = k * ISSUE_UNROLL + u
            _row_copy(x_ref, r, xs_ref, dest_ref[base + r], sem).start(priority=u % 2)
        return carry

    lax.fori_loop(0, tm // ISSUE_UNROLL, start, 0)
    pltpu.make_async_copy(x_ref, xs_ref.at[pl.ds(0, tm)], sem).wait()

    @pl.when(i == pl.num_programs(0) - 1)
    def _():
        zero_scr[...] = jnp.zeros_like(zero_scr)
        for g in range(N_GROUPS + 1):
            first = fill_start_ref[g]

            def start_fill(r, carry):
                _row_copy(zero_scr, 0, xs_ref, first + r, sem).start()
                return carry

            def wait_fill(r, carry):
                _row_copy(zero_scr, 0, xs_ref, 0, sem).wait()
                return carry

            lax.fori_loop(0, fill_n_ref[g], start_fill, 0)
            lax.fori_loop(0, fill_n_ref[g], wait_fill, 0)


def _scatter_call(dest, fill_start, fill_n, x1t, p_rows):
    n = x1t.shape[0]
    tm = TM_PERM
    grid_spec = pltpu.PrefetchScalarGridSpec(
        num_scalar_prefetch=3,
        grid=(n // tm,),
        in_specs=[pl.BlockSpec((tm, SUBLANES, LANES), lambda i, *_: (i, 0, 0))],
        out_specs=pl.BlockSpec(memory_space=pl.ANY),
        scratch_shapes=[pltpu.VMEM((1, SUBLANES, LANES), F32), pltpu.SemaphoreType.DMA(())],
    )
    return pl.pallas_call(
        _scatter_kernel,
        grid_spec=grid_spec,
        out_shape=jax.ShapeDtypeStruct((p_rows, SUBLANES, LANES), F32),
        compiler_params=pltpu.CompilerParams(dimension_semantics=("arbitrary",),
                                             vmem_limit_bytes=VMEM_LIMIT),
        name="row_scatter",
    )(dest, fill_start, fill_n, x1t)


def _moe_norm(xs_ref, g2_ref, x1_scr, hb_scr):
    t = xs_ref.shape[0]
    x1 = xs_ref[...].reshape(t, D_MODEL)
    h2 = _ffn_norm(x1, g2_ref)
    x1_scr[...] = x1
    hb_scr[...] = h2.astype(BF16)
    return h2


def _moe_route(h2, group, wrh_ref, wrl_ref, br_ref, cw_scr):
    t = h2.shape[0]
    rowi = lax.broadcasted_iota(jnp.int32, (SUBLANES, t), 0).astype(F32)
    neg = jnp.float32(-jnp.inf)
    logits = _router_logits(wrh_ref, wrl_ref, br_ref, h2)
    gl = jnp.where(rowi < N_GROUPS, logits[0:SUBLANES, :], neg)
    mine = jnp.sum(jnp.where(rowi == group.astype(F32), gl, 0.0), axis=0, keepdims=True)
    g_w = 1.0 / jnp.sum(jnp.exp(gl - mine), axis=0, keepdims=True)
    el = logits[SUBLANES:2 * SUBLANES, :]
    t1, i1 = _first_argmax(el, rowi)
    t2, i2 = _first_argmax(jnp.where(rowi == i1, neg, el), rowi)
    e = jnp.exp(t2 - t1)
    w1 = g_w / (1.0 + e)
    w2 = g_w * e / (1.0 + e)
    cw_t = jnp.where(rowi == i1, w1, 0.0) + jnp.where(rowi == i2, w2, 0.0)
    cw_scr[...] = jnp.concatenate([cw_t, jnp.zeros((LANES - SUBLANES, t), F32)], axis=0).T


def _moe_kernel(tg_ref, tv_ref, xs_ref, nxt_ref, g2_ref, wrh_ref, wrl_ref, br_ref, nrh_ref, nrl_ref,
                nbr_ref, wg_ref, wu_ref, wd_ref, o_ref, x1_scr, hb_scr, cw_scr):
    i = pl.program_id(0)
    t = xs_ref.shape[0]

    @pl.when(i == 0)
    def _():
        h2 = _moe_norm(xs_ref, g2_ref, x1_scr.at[0], hb_scr.at[0])
        _moe_route(h2, tg_ref[0], wrh_ref, wrl_ref, br_ref, cw_scr.at[0])

    @pl.when(tv_ref[i] == 0)
    def _():
        o_ref[...] = jnp.zeros_like(o_ref)

    def step(cur, nxt):
        following = jnp.minimum(i + 1, pl.num_programs(0) - 1)
        h2_next = _moe_norm(nxt_ref, g2_ref, x1_scr.at[nxt], hb_scr.at[nxt])
        hb, cw = hb_scr.at[cur], cw_scr.at[cur]
        gate_up = lambda ex: (_dot(hb[...], wg_ref[ex]), _dot(hb[...], wu_ref[ex]))
        acc = x1_scr[cur]
        ahead = gate_up(0)
        for ex in range(E_PER_GROUP):
            gate, up = ahead
            if ex + 1 < E_PER_GROUP:
                ahead = gate_up(ex + 1)
            act = (gate * jax.nn.sigmoid(gate)) * up * cw[:, ex:ex + 1]
            acc = acc + _dot(act.astype(BF16), wd_ref[ex])
            if ex == E_PER_GROUP // 2 - 1:
                _moe_route(h2_next, tg_ref[following], nrh_ref, nrl_ref, nbr_ref, cw_scr.at[nxt])
        o_ref[...] = acc.reshape(t, SUBLANES, LANES)

    for parity in range(2):
        @pl.when((tv_ref[i] > 0) & (lax.rem(i, 2) == parity))
        def _():
            step(parity, 1 - parity)


def _moe_call(tile_group, tile_valid, xs, g2, wrh, wrl, br, wg, wu, wd):
    p = xs.shape[0]
    t = TM_MOE
    last = p // t - 1
    by_block = lambda i, tg, tv: (i, 0, 0)
    next_block = lambda i, tg, tv: (jnp.minimum(i + 1, last), 0, 0)
    by_group = lambda i, tg, tv: (tg[i], 0, 0)
    next_group = lambda i, tg, tv: (tg[jnp.minimum(i + 1, last)], 0, 0)
    once = pl.Buffered(1)
    router = lambda index_map: [pl.BlockSpec((1, D_MODEL, LANES), index_map, pipeline_mode=once),
                                pl.BlockSpec((1, D_MODEL, LANES), index_map, pipeline_mode=once),
                                pl.BlockSpec((1, 1, LANES), index_map, pipeline_mode=once)]
    grid_spec = pltpu.PrefetchScalarGridSpec(
        num_scalar_prefetch=2,
        grid=(p // t,),
        in_specs=[pl.BlockSpec((t, SUBLANES, LANES), by_block),
                  pl.BlockSpec((t, SUBLANES, LANES), next_block),
                  pl.BlockSpec((1, D_MODEL), lambda i, *_: (0, 0))]
                 + router(by_group) + router(next_group)
                 + [pl.BlockSpec((E_PER_GROUP, D_MODEL, D_EXPERT), by_group, pipeline_mode=once),
                    pl.BlockSpec((E_PER_GROUP, D_MODEL, D_EXPERT), by_group, pipeline_mode=once),
                    pl.BlockSpec((E_PER_GROUP, D_EXPERT, D_MODEL), by_group, pipeline_mode=once)],
        out_specs=pl.BlockSpec((t, SUBLANES, LANES), by_block),
        scratch_shapes=[pltpu.VMEM((2, t, D_MODEL), F32), pltpu.VMEM((2, t, D_MODEL), BF16),
                        pltpu.VMEM((2, t, LANES), F32)],
    )
    return pl.pallas_call(
        _moe_kernel,
        grid_spec=grid_spec,
        out_shape=jax.ShapeDtypeStruct((p, SUBLANES, LANES), F32),
        compiler_params=pltpu.CompilerParams(dimension_semantics=("arbitrary",),
                                             vmem_limit_bytes=VMEM_LIMIT),
        name="moe_experts",
    )(tile_group, tile_valid, xs, xs, g2, wrh, wrl, br, wrh, wrl, br, wg, wu, wd)


def _gather_kernel(dest_ref, ys_ref, o_ref, rows_scr, sem):
    i = pl.program_id(0)
    tm = o_ref.shape[0]
    base = i * tm

    def start(k, carry):
        for u in range(ISSUE_UNROLL):
            r = k * ISSUE_UNROLL + u
            _row_copy(ys_ref, dest_ref[base + r], rows_scr, r, sem).start(priority=u % 2)
        return carry

    lax.fori_loop(0, tm // ISSUE_UNROLL, start, 0)
    pltpu.make_async_copy(ys_ref.at[pl.ds(0, tm)], rows_scr, sem).wait()
    o_ref[...] = rows_scr[...].reshape(tm, D_MODEL)


def _gather_call(dest, ys, n):
    tm = TM_PERM
    grid_spec = pltpu.PrefetchScalarGridSpec(
        num_scalar_prefetch=1,
        grid=(n // tm,),
        in_specs=[pl.BlockSpec(memory_space=pl.ANY)],
        out_specs=pl.BlockSpec((tm, D_MODEL), lambda i, *_: (i, 0)),
        scratch_shapes=[pltpu.VMEM((tm, SUBLANES, LANES), F32), pltpu.SemaphoreType.DMA(())],
    )
    return pl.pallas_call(
        _gather_kernel,
        grid_spec=grid_spec,
        out_shape=jax.ShapeDtypeStruct((n, D_MODEL), F32),
        compiler_params=pltpu.CompilerParams(dimension_semantics=("arbitrary",),
                                             vmem_limit_bytes=VMEM_LIMIT),
        name="row_gather",
    )(dest, ys)


def _router_slabs(w_rg, b_rg, w_re, b_re):
    def slabs(group_part, expert_part):
        lead = expert_part.shape[:-1]
        pads = [jnp.zeros(lead + (width,), F32)
                for width in (SUBLANES - N_GROUPS, LANES - SUBLANES - E_PER_GROUP)]
        return jnp.concatenate([jnp.broadcast_to(group_part, lead + (N_GROUPS,)), pads[0],
                                expert_part, pads[1]], axis=-1)

    w = slabs(w_rg, w_re.reshape(D_MODEL, N_GROUPS, E_PER_GROUP).transpose(1, 0, 2))
    b = slabs(b_rg, b_re.reshape(N_GROUPS, 1, E_PER_GROUP))
    hi = w.astype(BF16)
    lo = (w - hi.astype(F32)).astype(BF16)
    return hi, lo, b


def _layer(x2d, batch, seq_len, norm_mix_g, w_in, q_norm_g, k_norm_g, conv_w, w_sb_branch,
           w_conv_branch, w_out, norm_ffn_g, w_router_group, b_router_group, w_router_expert,
           b_router_expert, w_gate_e, w_up_e, w_down_e):
    n = x2d.shape[0]
    lane = jnp.arange(WIDTH)
    head_mean = jnp.where(lane[:, None] // HEAD_DIM == lane[None, :] // HEAD_DIM,
                          1.0 / HEAD_DIM, 0.0).astype(BF16)
    qg = (jnp.tile(q_norm_g, N_HEADS) * HEAD_DIM ** -0.5)[None, :]
    kg = jnp.tile(k_norm_g, N_HEADS)[None, :]
    qt, k, vt, y, sa, sb = _inproj_call(x2d, norm_mix_g[None, :], w_in.astype(BF16), qg, kg,
                                        head_mean, conv_w, seq_len)

    pos = jnp.arange(TK)
    tri = (pos[None, :] >= pos[:, None]).astype(BF16)
    attn, (wg16, wu16, wd16) = _attn_call(qt, k, vt, tri, (w_gate_e, w_up_e, w_down_e), batch,
                                          seq_len)

    wrh, wrl, br = _router_slabs(w_router_group, b_router_group, w_router_expert, b_router_expert)
    tpos = jnp.arange(TM_PROJ)
    upper = (tpos[:, None] < tpos[None, :]).astype(BF16)
    x1t, route, cnt = _merge_call(attn, y, sa, sb, x2d, w_sb_branch.astype(BF16),
                                  w_conv_branch.astype(BF16), w_out.astype(BF16),
                                  norm_ffn_g[None, :], wrh, wrl, br, upper)

    t = TM_MOE
    grp = route[0].astype(jnp.int32)
    rank = route[1].astype(jnp.int32)
    counts = cnt[0:N_GROUPS, 0].astype(jnp.int32)
    padded = ((counts + t - 1) // t) * t
    ends = jnp.cumsum(padded)
    offs = ends - padded
    dest = rank
    for g in range(N_GROUPS):
        dest = dest + jnp.where(grp == g, offs[g], 0)
    n_tiles = n // t + N_GROUPS
    tile_start = jnp.arange(n_tiles, dtype=jnp.int32) * t
    tile_valid = (tile_start < ends[-1]).astype(jnp.int32)
    tile_group = jnp.zeros_like(tile_start)
    for g in range(N_GROUPS - 1):
        tile_group = tile_group + (jnp.minimum(tile_start, ends[-1] - t) >= ends[g]).astype(jnp.int32)
    p_rows = n_tiles * t
    fill_start = jnp.concatenate([offs + counts, ends[-1:]])
    fill_n = jnp.concatenate([padded - counts, p_rows - ends[-1:]])

    xs = _scatter_call(dest, fill_start, fill_n, x1t, p_rows)
    ys = _moe_call(tile_group, tile_valid, xs, norm_ffn_g[None, :], wrh, wrl, br, wg16, wu16, wd16)
    return _gather_call(dest, ys, n)


def kernel(x, norm_mix_g, w_in, q_norm_g, k_norm_g, conv_w, w_sb_branch, w_conv_branch, w_out,
           norm_ffn_g, w_router_group, b_router_group, w_router_expert, b_router_expert,
           w_gate_e, w_up_e, w_down_e):
    batch, seq_len, d = x.shape
    x2d = x.reshape(batch * seq_len, d)
    for l in range(norm_mix_g.shape[0]):
        x2d = _layer(x2d, batch, seq_len, norm_mix_g[l], w_in[l], q_norm_g[l], k_norm_g[l],
                     conv_w[l], w_sb_branch[l], w_conv_branch[l], w_out[l], norm_ffn_g[l],
                     w_router_group[l], b_router_group[l], w_router_expert[l], b_router_expert[l],
                     w_gate_e[l], w_up_e[l], w_down_e[l])
    return x2d.reshape(batch, seq_len, d)
```

```python
import functools

import jax
import jax.numpy as jnp
from jax import lax
from jax.experimental import pallas as pl
from jax.experimental.pallas import tpu as pltpu

F32 = jnp.float32
BF16 = jnp.bfloat16

D_MODEL = 1024
N_HEADS = 8
HEAD_DIM = 64
WIDTH = N_HEADS * HEAD_DIM
CONV_K = 3
N_GROUPS = 4
E_PER_GROUP = 8
N_EXPERTS = N_GROUPS * E_PER_GROUP
D_EXPERT = 256
EPS = 1e-6

LANES = 128
SUBLANES = 8
HEADS_PER_STEP = LANES // HEAD_DIM
N_PAIRS = N_HEADS // HEADS_PER_STEP

TM_PROJ = 512
TQ = 256
TK = 256
ROW_BAND = 256
TM_MOE = 512
TM_PERM = 4096
ISSUE_UNROLL = 8
VMEM_LIMIT = 56 * 1024 * 1024

def _dot(a, b):
    return jnp.dot(a, b, preferred_element_type=F32)


def _split_bf16(v):
    hi = v.astype(BF16)
    lo = (v - hi.astype(F32)).astype(BF16)
    return hi, lo


def _inproj_kernel(x_ref, g1_ref, w_ref, qg_ref, kg_ref, pm_ref, cw_ref,
                   qt_ref, k_ref, vt_ref, y_ref, sa_ref, sb_ref, cu_scr, *, tiles_per_seq):
    tm = x_ref.shape[0]

    @pl.when(pl.program_id(0) % tiles_per_seq == 0)
    def _():
        cu_scr[0:SUBLANES, :] = jnp.zeros((SUBLANES, WIDTH), F32)

    bands = [pl.ds(b * ROW_BAND, ROW_BAND) for b in range(tm // ROW_BAND)]
    h_bands = []
    for rows in bands:
        x = x_ref[rows, :]
        ms = jnp.mean(x * x, axis=-1, keepdims=True)
        h_bands.append((x * lax.rsqrt(ms + EPS) * g1_ref[...]).astype(BF16))

    def proj(lo, width):
        return jnp.concatenate([_dot(h, w_ref[:, lo:lo + width]) for h in h_bands], axis=0)

    def head_norm(t, g_ref):
        hms = _dot((t * t).astype(BF16), pm_ref[...])
        return t * lax.rsqrt(hms + EPS) * g_ref[...]

    sa_ref[...] = jax.nn.sigmoid(proj(6 * WIDTH, D_MODEL))
    sb_ref[...] = jax.nn.sigmoid(proj(6 * WIDTH + D_MODEL, D_MODEL))

    c_b = proj(3 * WIDTH, WIDTH)
    cu = proj(4 * WIDTH, WIDTH) * proj(5 * WIDTH, WIDTH)
    cu_scr[SUBLANES:SUBLANES + tm, :] = cu
    cw = cw_ref[...]
    conv = (cw[0:1, :] * cu_scr[SUBLANES - 2:SUBLANES - 2 + tm, :]
            + cw[1:2, :] * cu_scr[SUBLANES - 1:SUBLANES - 1 + tm, :]
            + cw[2:3, :] * cu)
    y_ref[...] = (c_b * conv).astype(BF16)
    cu_scr[0:SUBLANES, :] = cu_scr[tm:tm + SUBLANES, :]

    q = head_norm(proj(0, WIDTH), qg_ref)
    k_ref[...] = head_norm(proj(WIDTH, WIDTH), kg_ref).astype(BF16)
    v = proj(2 * WIDTH, WIDTH)
    for blk in range(tm // TQ):
        qt_ref[blk] = q[blk * TQ:(blk + 1) * TQ, :].T.astype(BF16)
        vt_ref[blk] = v[blk * TK:(blk + 1) * TK, :].T.astype(BF16)


def _inproj_call(x2d, g1, w_in, qg, kg, pm, conv_w, seq_len):
    n = x2d.shape[0]
    tm = TM_PROJ
    in_w = w_in.shape[1]
    const = lambda shape: pl.BlockSpec(shape, lambda i: (0, 0), pipeline_mode=pl.Buffered(1))
    rows = lambda width: pl.BlockSpec((tm, width), lambda i: (i, 0))
    assert TQ == TK and tm % TQ == 0
    cols = pl.BlockSpec((tm // TQ, WIDTH, TQ), lambda i: (i, 0, 0))
    cols_shape = jax.ShapeDtypeStruct((n // TQ, WIDTH, TQ), BF16)
    return pl.pallas_call(
        functools.partial(_inproj_kernel, tiles_per_seq=seq_len // tm),
        grid=(n // tm,),
        in_specs=[rows(D_MODEL), const((1, D_MODEL)), const((D_MODEL, in_w)), const((1, WIDTH)),
                  const((1, WIDTH)), const((WIDTH, WIDTH)), const((CONV_K, WIDTH))],
        out_specs=[cols, rows(WIDTH), cols, rows(WIDTH), rows(D_MODEL), rows(D_MODEL)],
        out_shape=[cols_shape, jax.ShapeDtypeStruct((n, WIDTH), BF16)] * 2
                  + [jax.ShapeDtypeStruct((n, D_MODEL), F32)] * 2,
        scratch_shapes=[pltpu.VMEM((SUBLANES + tm, WIDTH), F32)],
        compiler_params=pltpu.CompilerParams(dimension_semantics=("arbitrary",),
                                             vmem_limit_bytes=VMEM_LIMIT),
        name="inproj",
    )(x2d, g1, w_in, qg, kg, pm, conv_w)


def _softplus(z):
    return jnp.maximum(z, 0.0) + jnp.log(1.0 + jnp.exp(-jnp.abs(z)))


def _attn_kernel(qt_ref, k_ref, vt_ref, tri_ref, wg_ref, wu_ref, wd_ref, o_ref, wg16_ref, wu16_ref,
                 wd16_ref, qm_scr, z_scr, zc_scr, suf_scr, a_scr, acc_scr, c_scr):
    i = pl.program_id(1)
    tq = qt_ref.shape[2]
    feat = lax.broadcasted_iota(jnp.int32, (LANES, tq), 0)
    first_head = feat < HEAD_DIM
    for p in range(N_PAIRS):
        q2 = qt_ref[0, p * LANES:(p + 1) * LANES, :]
        zero = jnp.zeros_like(q2)
        qm_scr[2 * p] = jnp.where(first_head, q2, zero)
        qm_scr[2 * p + 1] = jnp.where(first_head, zero, q2)
    acc_scr[...] = jnp.zeros_like(acc_scr)
    c_scr[...] = jnp.zeros_like(c_scr)

    def pair(s):
        p = s // HEADS_PER_STEP
        return slice(p * LANES, (p + 1) * LANES)

    def scores(s, j):
        keys = k_ref[pl.ds(pl.multiple_of(j * TK, TK), TK), pair(s)]
        z_scr[s] = _dot(keys, qm_scr[s])

    def suffix_sums(s, mask):
        z = z_scr[s]
        zc_scr[s] = z - c_scr[s]
        sp = _softplus(z.astype(BF16))
        if mask is not None:
            sp = jnp.where(mask, sp, jnp.zeros_like(sp))
        suf_scr[s] = _dot(tri_ref[...], sp)

    def weights(s, mask):
        suffix = suf_scr[s]
        a = jnp.exp(zc_scr[s] - suffix)
        if mask is not None:
            a = jnp.where(mask, a, 0.0)
        a_scr[s] = a.astype(BF16)
        c_scr[s] = c_scr[s] + suffix[0:1, :]

    def accumulate(s, j):
        acc_scr[s] = acc_scr[s] + _dot(vt_ref[j, pair(s), :], a_scr[s])

    key = lax.broadcasted_iota(jnp.int32, (TK, tq), 0)
    qry = lax.broadcasted_iota(jnp.int32, (TK, tq), 1)
    causal = key < qry
    nxt = jnp.maximum(i - 1, 0)
    for s in range(N_HEADS):
        scores(s, i)
    for s in range(N_HEADS):
        suffix_sums(s, causal)
        scores(s, nxt)
    for s in range(N_HEADS):
        weights(s, causal)

    def body(t, carry):
        j = i - t
        nxt = jnp.maximum(j - 1, 0)
        for s in range(N_HEADS):
            accumulate(s, j + 1)
            suffix_sums(s, None)
            scores(s, nxt)
        for s in range(N_HEADS):
            weights(s, None)
        return carry

    lax.fori_loop(1, i + 1, body, 0)
    for s in range(N_HEADS):
        accumulate(s, 0)
    for p in range(N_PAIRS):
        both = jnp.where(first_head, acc_scr[2 * p], acc_scr[2 * p + 1])
        o_ref[:, p * LANES:(p + 1) * LANES] = both.T.astype(o_ref.dtype)

    wg16_ref[...] = wg_ref[...].astype(BF16)
    wu16_ref[...] = wu_ref[...].astype(BF16)
    wd16_ref[...] = wd_ref[...].astype(BF16)


def _attn_call(qt, k, vt, tri, expert_weights, batch, seq_len):
    assert TQ == TK
    nq = seq_len // TQ
    steps = batch * nq
    qspec = pl.BlockSpec((1, WIDTH, TQ), lambda b, i: (b * nq + i, 0, 0))
    kspec = pl.BlockSpec((seq_len, WIDTH), lambda b, i: (b, 0))
    vspec = pl.BlockSpec((seq_len // TK, WIDTH, TK), lambda b, i: (b, 0, 0))
    ospec = pl.BlockSpec((TQ, WIDTH), lambda b, i: (b * nq + i, 0))
    slabs = [w.reshape(steps, -1, w.shape[-1]) for w in expert_weights]
    wspecs = [pl.BlockSpec((1,) + s.shape[1:], lambda b, i: (b * nq + i, 0, 0)) for s in slabs]
    attn, *cast = pl.pallas_call(
        _attn_kernel,
        grid=(batch, nq),
        in_specs=[qspec, kspec, vspec, pl.BlockSpec((TK, TK), lambda b, i: (0, 0))] + wspecs,
        out_specs=[ospec] + wspecs,
        out_shape=[jax.ShapeDtypeStruct(k.shape, BF16)]
                  + [jax.ShapeDtypeStruct(s.shape, BF16) for s in slabs],
        scratch_shapes=[pltpu.VMEM((N_HEADS, LANES, TQ), BF16),
                        pltpu.VMEM((N_HEADS, TK, TQ), F32),
                        pltpu.VMEM((N_HEADS, TK, TQ), F32),
                        pltpu.VMEM((N_HEADS, TK, TQ), F32),
                        pltpu.VMEM((N_HEADS, TK, TQ), BF16),
                        pltpu.VMEM((N_HEADS, LANES, TQ), F32),
                        pltpu.VMEM((N_HEADS, 1, TQ), F32)],
        compiler_params=pltpu.CompilerParams(dimension_semantics=("arbitrary", "arbitrary"),
                                             vmem_limit_bytes=VMEM_LIMIT),
        name="sb_attn",
    )(qt, k, vt, tri, *slabs)
    return attn, [c.reshape(w.shape) for c, w in zip(cast, expert_weights)]


def _router_logits(w_hi_ref, w_lo_ref, b_ref, h):
    hh, hl = _split_bf16(h)
    logits = _dot(hh, w_hi_ref[0]) + _dot(hl, w_hi_ref[0]) + _dot(hh, w_lo_ref[0]) + b_ref[0]
    return logits.T


def _first_argmax(vals, rowi):
    top = jnp.max(vals, axis=0, keepdims=True)
    idx = jnp.min(jnp.where(vals == top, rowi, jnp.float32(SUBLANES)), axis=0, keepdims=True)
    return top, idx


def _ffn_norm(x1, g2_ref):
    ms = jnp.mean(x1 * x1, axis=-1, keepdims=True)
    return x1 * lax.rsqrt(ms + EPS) * g2_ref[...]


def _merge_kernel(attn_ref, y_ref, sa_ref, sb_ref, x_ref, wsb_ref, wcv_ref, wout_ref, g2_ref,
                  wrh_ref, wrl_ref, br_ref, upper_ref, x1t_ref, route_ref, cnt_ref, cnt_scr):
    tm = x_ref.shape[0]

    @pl.when(pl.program_id(0) == 0)
    def _():
        cnt_scr[...] = jnp.zeros_like(cnt_scr)

    bands = [pl.ds(b * ROW_BAND, ROW_BAND) for b in range(tm // ROW_BAND)]
    merged = [(sa_ref[rows, :] * _dot(attn_ref[rows, :], wsb_ref[...])
               + sb_ref[rows, :] * _dot(y_ref[rows, :], wcv_ref[...])).astype(BF16)
              for rows in bands]
    x1 = [x_ref[rows, :] + _dot(m, wout_ref[...]) for rows, m in zip(bands, merged)]
    logits = []
    for rows, x1_band in zip(bands, x1):
        x1t_ref[rows] = x1_band.reshape(ROW_BAND, SUBLANES, LANES)
        logits.append(_router_logits(wrh_ref, wrl_ref, br_ref, _ffn_norm(x1_band, g2_ref)))

    rowi = lax.broadcasted_iota(jnp.int32, (SUBLANES, tm), 0).astype(F32)
    gl = jnp.concatenate([lg[0:SUBLANES, :] for lg in logits], axis=1)
    gl = jnp.where(rowi < N_GROUPS, gl, jnp.float32(-jnp.inf))
    _, gidx = _first_argmax(gl, rowi)

    onehot = jnp.where(rowi == gidx, 1.0, 0.0)
    before = _dot(onehot.astype(BF16), upper_ref[...]) + cnt_scr[:, 0:1]
    rank = jnp.sum(onehot * before, axis=0, keepdims=True)
    cnt_scr[...] = cnt_scr[...] + jnp.sum(onehot, axis=1, keepdims=True)
    cnt_ref[...] = cnt_scr[...]
    route_ref[...] = jnp.where(rowi == 0, gidx, jnp.where(rowi == 1, rank, 0.0))


def _merge_call(attn, y, sa, sb, x2d, wsb, wcv, wout, g2, wrh, wrl, br, upper):
    n = x2d.shape[0]
    tm = TM_PROJ
    rows = lambda width: pl.BlockSpec((tm, width), lambda i: (i, 0))
    const = lambda shape: pl.BlockSpec(shape, lambda i: (0,) * len(shape),
                                       pipeline_mode=pl.Buffered(1))
    slab = (1, D_MODEL, LANES)
    return pl.pallas_call(
        _merge_kernel,
        grid=(n // tm,),
        in_specs=[rows(WIDTH), rows(WIDTH), rows(D_MODEL), rows(D_MODEL), rows(D_MODEL),
                  const((WIDTH, D_MODEL)), const((WIDTH, D_MODEL)), const((D_MODEL, D_MODEL)),
                  const((1, D_MODEL)), const(slab), const(slab), const((1, 1, LANES)),
                  const((tm, tm))],
        out_specs=[pl.BlockSpec((tm, SUBLANES, LANES), lambda i: (i, 0, 0)),
                   pl.BlockSpec((SUBLANES, tm), lambda i: (0, i)),
                   pl.BlockSpec((SUBLANES, LANES), lambda i: (0, 0))],
        out_shape=[jax.ShapeDtypeStruct((n, SUBLANES, LANES), F32),
                   jax.ShapeDtypeStruct((SUBLANES, n), F32),
                   jax.ShapeDtypeStruct((SUBLANES, LANES), F32)],
        scratch_shapes=[pltpu.VMEM((SUBLANES, LANES), F32)],
        compiler_params=pltpu.CompilerParams(dimension_semantics=("arbitrary",),
                                             vmem_limit_bytes=VMEM_LIMIT),
        name="merge_router",
    )(attn, y, sa, sb, x2d, wsb, wcv, wout, g2, wrh, wrl, br, upper)


def _row_copy(src_ref, src_row, dst_ref, dst_row, sem):
    return pltpu.make_async_copy(src_ref.at[src_row], dst_ref.at[dst_row], sem)


def _scatter_kernel(dest_ref, fill_start_ref, fill_n_ref, x_ref, xs_ref, zero_scr, sem):
    i = pl.program_id(0)
    tm = x_ref.shape[0]
    base = i * tm

    def start(k, carry):
        for u in range(ISSUE_UNROLL):
            r = k * ISSUE_UNROLL + u
            _row_copy(x_ref, r, xs_ref, dest_ref[base + r], sem).start(priority=u % 2)
        return carry

    lax.fori_loop(0, tm // ISSUE_UNROLL, start, 0)
    pltpu.make_async_copy(x_ref, xs_ref.at[pl.ds(0, tm)], sem).wait()

    @pl.when(i == pl.num_programs(0) - 1)
    def _():
        zero_scr[...] = jnp.zeros_like(zero_scr)
        for g in range(N_GROUPS + 1):
            first = fill_start_ref[g]

            def start_fill(r, carry):
                _row_copy(zero_scr, 0, xs_ref, first + r, sem).start()
                return carry

            def wait_fill(r, carry):
                _row_copy(zero_scr, 0, xs_ref, 0, sem).wait()
                return carry

            lax.fori_loop(0, fill_n_ref[g], start_fill, 0)
            lax.fori_loop(0, fill_n_ref[g], wait_fill, 0)


def _scatter_call(dest, fill_start, fill_n, x1t, p_rows):
    n = x1t.shape[0]
    tm = TM_PERM
    grid_spec = pltpu.PrefetchScalarGridSpec(
        num_scalar_prefetch=3,
        grid=(n // tm,),
        in_specs=[pl.BlockSpec((tm, SUBLANES, LANES), lambda i, *_: (i, 0, 0))],
        out_specs=pl.BlockSpec(memory_space=pl.ANY),
        scratch_shapes=[pltpu.VMEM((1, SUBLANES, LANES), F32), pltpu.SemaphoreType.DMA(())],
    )
    return pl.pallas_call(
        _scatter_kernel,
        grid_spec=grid_spec,
        out_shape=jax.ShapeDtypeStruct((p_rows, SUBLANES, LANES), F32),
        compiler_params=pltpu.CompilerParams(dimension_semantics=("arbitrary",),
                                             vmem_limit_bytes=VMEM_LIMIT),
        name="row_scatter",
    )(dest, fill_start, fill_n, x1t)


def _moe_kernel(tg_ref, tv_ref, xs_ref, g2_ref, wrh_ref, wrl_ref, br_ref, wg_ref, wu_ref, wd_ref,
                o_ref):
    i = pl.program_id(0)
    t = xs_ref.shape[0]

    @pl.when(tv_ref[i] == 0)
    def _():
        o_ref[...] = jnp.zeros_like(o_ref)

    @pl.when(tv_ref[i] > 0)
    def _():
        x1 = xs_ref[...].reshape(t, D_MODEL)
        h2 = _ffn_norm(x1, g2_ref)

        rowi = lax.broadcasted_iota(jnp.int32, (SUBLANES, t), 0).astype(F32)
        neg = jnp.float32(-jnp.inf)
        logits = _router_logits(wrh_ref, wrl_ref, br_ref, h2)
        gl = jnp.where(rowi < N_GROUPS, logits[0:SUBLANES, :], neg)
        mine = jnp.sum(jnp.where(rowi == tg_ref[i].astype(F32), gl, 0.0), axis=0, keepdims=True)
        g_w = 1.0 / jnp.sum(jnp.exp(gl - mine), axis=0, keepdims=True)
        el = logits[SUBLANES:2 * SUBLANES, :]
        t1, i1 = _first_argmax(el, rowi)
        t2, i2 = _first_argmax(jnp.where(rowi == i1, neg, el), rowi)
        e = jnp.exp(t2 - t1)
        w1 = g_w / (1.0 + e)
        w2 = g_w * e / (1.0 + e)
        cw_t = jnp.where(rowi == i1, w1, 0.0) + jnp.where(rowi == i2, w2, 0.0)
        cw = jnp.concatenate([cw_t, jnp.zeros((LANES - SUBLANES, t), F32)], axis=0).T

        hb = h2.astype(BF16)
        gate_up = lambda ex: (_dot(hb, wg_ref[ex]), _dot(hb, wu_ref[ex]))
        acc = x1
        ahead = gate_up(0)
        for ex in range(E_PER_GROUP):
            gate, up = ahead
            if ex + 1 < E_PER_GROUP:
                ahead = gate_up(ex + 1)
            act = (gate * jax.nn.sigmoid(gate)) * up * cw[:, ex:ex + 1]
            acc = acc + _dot(act.astype(BF16), wd_ref[ex])
        o_ref[...] = acc.reshape(t, SUBLANES, LANES)


def _moe_call(tile_group, tile_valid, xs, g2, wrh, wrl, br, wg, wu, wd):
    p = xs.shape[0]
    t = TM_MOE
    by_block = lambda i, tg, tv: (i, 0, 0)
    by_group = lambda i, tg, tv: (tg[i], 0, 0)
    once = pl.Buffered(1)
    grid_spec = pltpu.PrefetchScalarGridSpec(
        num_scalar_prefetch=2,
        grid=(p // t,),
        in_specs=[pl.BlockSpec((t, SUBLANES, LANES), by_block),
                  pl.BlockSpec((1, D_MODEL), lambda i, *_: (0, 0)),
                  pl.BlockSpec((1, D_MODEL, LANES), by_group, pipeline_mode=once),
                  pl.BlockSpec((1, D_MODEL, LANES), by_group, pipeline_mode=once),
                  pl.BlockSpec((1, 1, LANES), by_group, pipeline_mode=once),
                  pl.BlockSpec((E_PER_GROUP, D_MODEL, D_EXPERT), by_group, pipeline_mode=once),
                  pl.BlockSpec((E_PER_GROUP, D_MODEL, D_EXPERT), by_group, pipeline_mode=once),
                  pl.BlockSpec((E_PER_GROUP, D_EXPERT, D_MODEL), by_group, pipeline_mode=once)],
        out_specs=pl.BlockSpec((t, SUBLANES, LANES), by_block),
    )
    return pl.pallas_call(
        _moe_kernel,
        grid_spec=grid_spec,
        out_shape=jax.ShapeDtypeStruct((p, SUBLANES, LANES), F32),
        compiler_params=pltpu.CompilerParams(dimension_semantics=("arbitrary",),
                                             vmem_limit_bytes=VMEM_LIMIT),
        name="moe_experts",
    )(tile_group, tile_valid, xs, g2, wrh, wrl, br, wg, wu, wd)


def _gather_kernel(dest_ref, ys_ref, o_ref, rows_scr, sem):
    i = pl.program_id(0)
    tm = o_ref.shape[0]
    base = i * tm

    def start(k, carry):
        for u in range(ISSUE_UNROLL):
            r = k * ISSUE_UNROLL + u
            _row_copy(ys_ref, dest_ref[base + r], rows_scr, r, sem).start(priority=u % 2)
        return carry

    lax.fori_loop(0, tm // ISSUE_UNROLL, start, 0)
    pltpu.make_async_copy(ys_ref.at[pl.ds(0, tm)], rows_scr, sem).wait()
    o_ref[...] = rows_scr[...].reshape(tm, D_MODEL)


def _gather_call(dest, ys, n):
    tm = TM_PERM
    grid_spec = pltpu.PrefetchScalarGridSpec(
        num_scalar_prefetch=1,
        grid=(n // tm,),
        in_specs=[pl.BlockSpec(memory_space=pl.ANY)],
        out_specs=pl.BlockSpec((tm, D_MODEL), lambda i, *_: (i, 0)),
        scratch_shapes=[pltpu.VMEM((tm, SUBLANES, LANES), F32), pltpu.SemaphoreType.DMA(())],
    )
    return pl.pallas_call(
        _gather_kernel,
        grid_spec=grid_spec,
        out_shape=jax.ShapeDtypeStruct((n, D_MODEL), F32),
        compiler_params=pltpu.CompilerParams(dimension_semantics=("arbitrary",),
                                             vmem_limit_bytes=VMEM_LIMIT),
        name="row_gather",
    )(dest, ys)


def _router_slabs(w_rg, b_rg, w_re, b_re):
    def slabs(group_part, expert_part):
        lead = expert_part.shape[:-1]
        pads = [jnp.zeros(lead + (width,), F32)
                for width in (SUBLANES - N_GROUPS, LANES - SUBLANES - E_PER_GROUP)]
        return jnp.concatenate([jnp.broadcast_to(group_part, lead + (N_GROUPS,)), pads[0],
                                expert_part, pads[1]], axis=-1)

    w = slabs(w_rg, w_re.reshape(D_MODEL, N_GROUPS, E_PER_GROUP).transpose(1, 0, 2))
    b = slabs(b_rg, b_re.reshape(N_GROUPS, 1, E_PER_GROUP))
    hi = w.astype(BF16)
    lo = (w - hi.astype(F32)).astype(BF16)
    return hi, lo, b


def _layer(x2d, batch, seq_len, norm_mix_g, w_in, q_norm_g, k_norm_g, conv_w, w_sb_branch,
           w_conv_branch, w_out, norm_ffn_g, w_router_group, b_router_group, w_router_expert,
           b_router_expert, w_gate_e, w_up_e, w_down_e):
    n = x2d.shape[0]
    lane = jnp.arange(WIDTH)
    head_mean = jnp.where(lane[:, None] // HEAD_DIM == lane[None, :] // HEAD_DIM,
                          1.0 / HEAD_DIM, 0.0).astype(BF16)
    qg = (jnp.tile(q_norm_g, N_HEADS) * HEAD_DIM ** -0.5)[None, :]
    kg = jnp.tile(k_norm_g, N_HEADS)[None, :]
    qt, k, vt, y, sa, sb = _inproj_call(x2d, norm_mix_g[None, :], w_in.astype(BF16), qg, kg,
                                        head_mean, conv_w, seq_len)

    pos = jnp.arange(TK)
    tri = (pos[None, :] >= pos[:, None]).astype(BF16)
    attn, (wg16, wu16, wd16) = _attn_call(qt, k, vt, tri, (w_gate_e, w_up_e, w_down_e), batch,
                                          seq_len)

    wrh, wrl, br = _router_slabs(w_router_group, b_router_group, w_router_expert, b_router_expert)
    tpos = jnp.arange(TM_PROJ)
    upper = (tpos[:, None] < tpos[None, :]).astype(BF16)
    x1t, route, cnt = _merge_call(attn, y, sa, sb, x2d, w_sb_branch.astype(BF16),
                                  w_conv_branch.astype(BF16), w_out.astype(BF16),
                                  norm_ffn_g[None, :], wrh, wrl, br, upper)

    t = TM_MOE
    grp = route[0].astype(jnp.int32)
    rank = route[1].astype(jnp.int32)
    counts = cnt[0:N_GROUPS, 0].astype(jnp.int32)
    padded = ((counts + t - 1) // t) * t
    ends = jnp.cumsum(padded)
    offs = ends - padded
    dest = rank
    for g in range(N_GROUPS):
        dest = dest + jnp.where(grp == g, offs[g], 0)
    n_tiles = n // t + N_GROUPS
    tile_start = jnp.arange(n_tiles, dtype=jnp.int32) * t
    tile_valid = (tile_start < ends[-1]).astype(jnp.int32)
    tile_group = jnp.zeros_like(tile_start)
    for g in range(N_GROUPS - 1):
        tile_group = tile_group + (jnp.minimum(tile_start, ends[-1] - t) >= ends[g]).astype(jnp.int32)
    p_rows = n_tiles * t
    fill_start = jnp.concatenate([offs + counts, ends[-1:]])
    fill_n = jnp.concatenate([padded - counts, p_rows - ends[-1:]])

    xs = _scatter_call(dest, fill_start, fill_n, x1t, p_rows)
    ys = _moe_call(tile_group, tile_valid, xs, norm_ffn_g[None, :], wrh, wrl, br, wg16, wu16, wd16)
    return _gather_call(dest, ys, n)


def kernel(x, norm_mix_g, w_in, q_norm_g, k_norm_g, conv_w, w_sb_branch, w_conv_branch, w_out,
           norm_ffn_g, w_router_group, b_router_group, w_router_expert, b_router_expert,
           w_gate_e, w_up_e, w_down_e):
    batch, seq_len, d = x.shape
    x2d = x.reshape(batch * seq_len, d)
    for l in range(norm_mix_g.shape[0]):
        x2d = _layer(x2d, batch, seq_len, norm_mix_g[l], w_in[l], q_norm_g[l], k_norm_g[l],
                     conv_w[l], w_sb_branch[l], w_conv_branch[l], w_out[l], norm_ffn_g[l],
                     w_router_group[l], b_router_group[l], w_router_expert[l], b_router_expert[l],
                     w_gate_e[l], w_up_e[l], w_down_e[l])
    return x2d.reshape(batch, seq_len, d)
```

```python
import functools

import jax
import jax.numpy as jnp
from jax import lax
from jax.experimental import pallas as pl
from jax.experimental.pallas import tpu as pltpu

F32 = jnp.float32
BF16 = jnp.bfloat16

D_MODEL = 1024
N_HEADS = 8
HEAD_DIM = 64
WIDTH = N_HEADS * HEAD_DIM
CONV_K = 3
N_GROUPS = 4
E_PER_GROUP = 8
N_EXPERTS = N_GROUPS * E_PER_GROUP
D_EXPERT = 256
EPS = 1e-6

LANES = 128
SUBLANES = 8
HEADS_PER_STEP = LANES // HEAD_DIM
N_PAIRS = N_HEADS // HEADS_PER_STEP

TM_PROJ = 1024
TQ = 256
TK = 256
ROW_BAND = 256
TM_MOE = 512
TM_PERM = 4096
ISSUE_UNROLL = 8
VMEM_LIMIT = 56 * 1024 * 1024

def _dot(a, b):
    return jnp.dot(a, b, preferred_element_type=F32)


def _split_bf16(v):
    hi = v.astype(BF16)
    lo = (v - hi.astype(F32)).astype(BF16)
    return hi, lo


def _inproj_kernel(x_ref, g1_ref, w_ref, qg_ref, kg_ref, pm_ref, cw_ref,
                   qt_ref, k_ref, vt_ref, y_ref, sa_ref, sb_ref, cu_scr, *, tiles_per_seq):
    tm = x_ref.shape[0]

    @pl.when(pl.program_id(0) % tiles_per_seq == 0)
    def _():
        cu_scr[0:SUBLANES, :] = jnp.zeros((SUBLANES, WIDTH), F32)

    bands = [pl.ds(b * ROW_BAND, ROW_BAND) for b in range(tm // ROW_BAND)]
    h_bands = []
    for rows in bands:
        x = x_ref[rows, :]
        ms = jnp.mean(x * x, axis=-1, keepdims=True)
        h_bands.append((x * lax.rsqrt(ms + EPS) * g1_ref[...]).astype(BF16))

    def proj(lo, width):
        return jnp.concatenate([_dot(h, w_ref[:, lo:lo + width]) for h in h_bands], axis=0)

    def head_norm(t, g_ref):
        hms = _dot((t * t).astype(BF16), pm_ref[...])
        return t * lax.rsqrt(hms + EPS) * g_ref[...]

    sa_ref[...] = jax.nn.sigmoid(proj(6 * WIDTH, D_MODEL))
    sb_ref[...] = jax.nn.sigmoid(proj(6 * WIDTH + D_MODEL, D_MODEL))

    c_b = proj(3 * WIDTH, WIDTH)
    cu = proj(4 * WIDTH, WIDTH) * proj(5 * WIDTH, WIDTH)
    cu_scr[SUBLANES:SUBLANES + tm, :] = cu
    cw = cw_ref[...]
    conv = (cw[0:1, :] * cu_scr[SUBLANES - 2:SUBLANES - 2 + tm, :]
            + cw[1:2, :] * cu_scr[SUBLANES - 1:SUBLANES - 1 + tm, :]
            + cw[2:3, :] * cu)
    y_ref[...] = (c_b * conv).astype(BF16)
    cu_scr[0:SUBLANES, :] = cu_scr[tm:tm + SUBLANES, :]

    q = head_norm(proj(0, WIDTH), qg_ref)
    k_ref[...] = head_norm(proj(WIDTH, WIDTH), kg_ref).astype(BF16)
    v = proj(2 * WIDTH, WIDTH)
    for blk in range(tm // TQ):
        qt_ref[blk] = q[blk * TQ:(blk + 1) * TQ, :].T.astype(BF16)
        vt_ref[blk] = v[blk * TK:(blk + 1) * TK, :].T.astype(BF16)


def _inproj_call(x2d, g1, w_in, qg, kg, pm, conv_w, seq_len):
    n = x2d.shape[0]
    tm = TM_PROJ
    in_w = w_in.shape[1]
    const = lambda shape: pl.BlockSpec(shape, lambda i: (0, 0), pipeline_mode=pl.Buffered(1))
    rows = lambda width: pl.BlockSpec((tm, width), lambda i: (i, 0))
    assert TQ == TK and tm % TQ == 0
    cols = pl.BlockSpec((tm // TQ, WIDTH, TQ), lambda i: (i, 0, 0))
    cols_shape = jax.ShapeDtypeStruct((n // TQ, WIDTH, TQ), BF16)
    return pl.pallas_call(
        functools.partial(_inproj_kernel, tiles_per_seq=seq_len // tm),
        grid=(n // tm,),
        in_specs=[rows(D_MODEL), const((1, D_MODEL)), const((D_MODEL, in_w)), const((1, WIDTH)),
                  const((1, WIDTH)), const((WIDTH, WIDTH)), const((CONV_K, WIDTH))],
        out_specs=[cols, rows(WIDTH), cols, rows(WIDTH), rows(D_MODEL), rows(D_MODEL)],
        out_shape=[cols_shape, jax.ShapeDtypeStruct((n, WIDTH), BF16)] * 2
                  + [jax.ShapeDtypeStruct((n, D_MODEL), F32)] * 2,
        scratch_shapes=[pltpu.VMEM((SUBLANES + tm, WIDTH), F32)],
        compiler_params=pltpu.CompilerParams(dimension_semantics=("arbitrary",),
                                             vmem_limit_bytes=VMEM_LIMIT),
        name="inproj",
    )(x2d, g1, w_in, qg, kg, pm, conv_w)


def _softplus(z):
    return jnp.maximum(z, 0.0) + jnp.log(1.0 + jnp.exp(-jnp.abs(z)))


def _attn_kernel(qt_ref, k_ref, vt_ref, tri_ref, wg_ref, wu_ref, wd_ref, o_ref, wg16_ref, wu16_ref,
                 wd16_ref, qm_scr, z_scr, zc_scr, suf_scr, a_scr, acc_scr, c_scr):
    i = pl.program_id(1)
    tq = qt_ref.shape[2]
    feat = lax.broadcasted_iota(jnp.int32, (LANES, tq), 0)
    first_head = feat < HEAD_DIM
    for p in range(N_PAIRS):
        q2 = qt_ref[0, p * LANES:(p + 1) * LANES, :]
        zero = jnp.zeros_like(q2)
        qm_scr[2 * p] = jnp.where(first_head, q2, zero)
        qm_scr[2 * p + 1] = jnp.where(first_head, zero, q2)
    acc_scr[...] = jnp.zeros_like(acc_scr)
    c_scr[...] = jnp.zeros_like(c_scr)

    def pair(s):
        p = s // HEADS_PER_STEP
        return slice(p * LANES, (p + 1) * LANES)

    def scores(s, j):
        keys = k_ref[pl.ds(pl.multiple_of(j * TK, TK), TK), pair(s)]
        z_scr[s] = _dot(keys, qm_scr[s])

    def suffix_sums(s, mask):
        z = z_scr[s]
        zc_scr[s] = z - c_scr[s]
        sp = _softplus(z.astype(BF16))
        if mask is not None:
            sp = jnp.where(mask, sp, jnp.zeros_like(sp))
        suf_scr[s] = _dot(tri_ref[...], sp)

    def weights(s, mask):
        suffix = suf_scr[s]
        a = jnp.exp(zc_scr[s] - suffix)
        if mask is not None:
            a = jnp.where(mask, a, 0.0)
        a_scr[s] = a.astype(BF16)
        c_scr[s] = c_scr[s] + suffix[0:1, :]

    def accumulate(s, j):
        acc_scr[s] = acc_scr[s] + _dot(vt_ref[j, pair(s), :], a_scr[s])

    key = lax.broadcasted_iota(jnp.int32, (TK, tq), 0)
    qry = lax.broadcasted_iota(jnp.int32, (TK, tq), 1)
    causal = key < qry
    nxt = jnp.maximum(i - 1, 0)
    for s in range(N_HEADS):
        scores(s, i)
    for s in range(N_HEADS):
        suffix_sums(s, causal)
        scores(s, nxt)
    for s in range(N_HEADS):
        weights(s, causal)

    def body(t, carry):
        j = i - t
        nxt = jnp.maximum(j - 1, 0)
        for s in range(N_HEADS):
            accumulate(s, j + 1)
            suffix_sums(s, None)
            scores(s, nxt)
        for s in range(N_HEADS):
            weights(s, None)
        return carry

    lax.fori_loop(1, i + 1, body, 0)
    for s in range(N_HEADS):
        accumulate(s, 0)
    for p in range(N_PAIRS):
        both = jnp.where(first_head, acc_scr[2 * p], acc_scr[2 * p + 1])
        o_ref[:, p * LANES:(p + 1) * LANES] = both.T.astype(o_ref.dtype)

    wg16_ref[...] = wg_ref[...].astype(BF16)
    wu16_ref[...] = wu_ref[...].astype(BF16)
    wd16_ref[...] = wd_ref[...].astype(BF16)


def _attn_call(qt, k, vt, tri, expert_weights, batch, seq_len):
    assert TQ == TK
    nq = seq_len // TQ
    steps = batch * nq
    qspec = pl.BlockSpec((1, WIDTH, TQ), lambda b, i: (b * nq + i, 0, 0))
    kspec = pl.BlockSpec((seq_len, WIDTH), lambda b, i: (b, 0))
    vspec = pl.BlockSpec((seq_len // TK, WIDTH, TK), lambda b, i: (b, 0, 0))
    ospec = pl.BlockSpec((TQ, WIDTH), lambda b, i: (b * nq + i, 0))
    slabs = [w.reshape(steps, -1, w.shape[-1]) for w in expert_weights]
    wspecs = [pl.BlockSpec((1,) + s.shape[1:], lambda b, i: (b * nq + i, 0, 0)) for s in slabs]
    attn, *cast = pl.pallas_call(
        _attn_kernel,
        grid=(batch, nq),
        in_specs=[qspec, kspec, vspec, pl.BlockSpec((TK, TK), lambda b, i: (0, 0))] + wspecs,
        out_specs=[ospec] + wspecs,
        out_shape=[jax.ShapeDtypeStruct(k.shape, BF16)]
                  + [jax.ShapeDtypeStruct(s.shape, BF16) for s in slabs],
        scratch_shapes=[pltpu.VMEM((N_HEADS, LANES, TQ), BF16),
                        pltpu.VMEM((N_HEADS, TK, TQ), F32),
                        pltpu.VMEM((N_HEADS, TK, TQ), F32),
                        pltpu.VMEM((N_HEADS, TK, TQ), F32),
                        pltpu.VMEM((N_HEADS, TK, TQ), BF16),
                        pltpu.VMEM((N_HEADS, LANES, TQ), F32),
                        pltpu.VMEM((N_HEADS, 1, TQ), F32)],
        compiler_params=pltpu.CompilerParams(dimension_semantics=("arbitrary", "arbitrary"),
                                             vmem_limit_bytes=VMEM_LIMIT),
        name="sb_attn",
    )(qt, k, vt, tri, *slabs)
    return attn, [c.reshape(w.shape) for c, w in zip(cast, expert_weights)]


def _router_logits(w_hi_ref, w_lo_ref, b_ref, h):
    hh, hl = _split_bf16(h)
    logits = _dot(hh, w_hi_ref[0]) + _dot(hl, w_hi_ref[0]) + _dot(hh, w_lo_ref[0]) + b_ref[0]
    return logits.T


def _first_argmax(vals, rowi):
    top = jnp.max(vals, axis=0, keepdims=True)
    idx = jnp.min(jnp.where(vals == top, rowi, jnp.float32(SUBLANES)), axis=0, keepdims=True)
    return top, idx


def _ffn_norm(x1, g2_ref):
    ms = jnp.mean(x1 * x1, axis=-1, keepdims=True)
    return x1 * lax.rsqrt(ms + EPS) * g2_ref[...]


def _merge_kernel(attn_ref, y_ref, sa_ref, sb_ref, x_ref, wsb_ref, wcv_ref, wout_ref, g2_ref,
                  wrh_ref, wrl_ref, br_ref, upper_ref, x1t_ref, route_ref, cnt_ref, cnt_scr):
    tm = x_ref.shape[0]

    @pl.when(pl.program_id(0) == 0)
    def _():
        cnt_scr[...] = jnp.zeros_like(cnt_scr)

    bands = [pl.ds(b * ROW_BAND, ROW_BAND) for b in range(tm // ROW_BAND)]
    merged = [(sa_ref[rows, :] * _dot(attn_ref[rows, :], wsb_ref[...])
               + sb_ref[rows, :] * _dot(y_ref[rows, :], wcv_ref[...])).astype(BF16)
              for rows in bands]
    x1 = [x_ref[rows, :] + _dot(m, wout_ref[...]) for rows, m in zip(bands, merged)]
    logits = []
    for rows, x1_band in zip(bands, x1):
        x1t_ref[rows] = x1_band.reshape(ROW_BAND, SUBLANES, LANES)
        logits.append(_router_logits(wrh_ref, wrl_ref, br_ref, _ffn_norm(x1_band, g2_ref)))

    rowi = lax.broadcasted_iota(jnp.int32, (SUBLANES, tm), 0).astype(F32)
    gl = jnp.concatenate([lg[0:SUBLANES, :] for lg in logits], axis=1)
    gl = jnp.where(rowi < N_GROUPS, gl, jnp.float32(-jnp.inf))
    _, gidx = _first_argmax(gl, rowi)

    onehot = jnp.where(rowi == gidx, 1.0, 0.0)
    before = _dot(onehot.astype(BF16), upper_ref[...]) + cnt_scr[:, 0:1]
    rank = jnp.sum(onehot * before, axis=0, keepdims=True)
    cnt_scr[...] = cnt_scr[...] + jnp.sum(onehot, axis=1, keepdims=True)
    cnt_ref[...] = cnt_scr[...]
    route_ref[...] = jnp.where(rowi == 0, gidx, jnp.where(rowi == 1, rank, 0.0))


def _merge_call(attn, y, sa, sb, x2d, wsb, wcv, wout, g2, wrh, wrl, br, upper):
    n = x2d.shape[0]
    tm = TM_PROJ
    rows = lambda width: pl.BlockSpec((tm, width), lambda i: (i, 0))
    const = lambda shape: pl.BlockSpec(shape, lambda i: (0,) * len(shape),
                                       pipeline_mode=pl.Buffered(1))
    slab = (1, D_MODEL, LANES)
    return pl.pallas_call(
        _merge_kernel,
        grid=(n // tm,),
        in_specs=[rows(WIDTH), rows(WIDTH), rows(D_MODEL), rows(D_MODEL), rows(D_MODEL),
                  const((WIDTH, D_MODEL)), const((WIDTH, D_MODEL)), const((D_MODEL, D_MODEL)),
                  const((1, D_MODEL)), const(slab), const(slab), const((1, 1, LANES)),
                  const((tm, tm))],
        out_specs=[pl.BlockSpec((tm, SUBLANES, LANES), lambda i: (i, 0, 0)),
                   pl.BlockSpec((SUBLANES, tm), lambda i: (0, i)),
                   pl.BlockSpec((SUBLANES, LANES), lambda i: (0, 0))],
        out_shape=[jax.ShapeDtypeStruct((n, SUBLANES, LANES), F32),
                   jax.ShapeDtypeStruct((SUBLANES, n), F32),
                   jax.ShapeDtypeStruct((SUBLANES, LANES), F32)],
        scratch_shapes=[pltpu.VMEM((SUBLANES, LANES), F32)],
        compiler_params=pltpu.CompilerParams(dimension_semantics=("arbitrary",),
                                             vmem_limit_bytes=VMEM_LIMIT),
        name="merge_router",
    )(attn, y, sa, sb, x2d, wsb, wcv, wout, g2, wrh, wrl, br, upper)


def _row_copy(src_ref, src_row, dst_ref, dst_row, sem):
    return pltpu.make_async_copy(src_ref.at[src_row], dst_ref.at[dst_row], sem)


def _scatter_kernel(dest_ref, fill_start_ref, fill_n_ref, x_ref, xs_ref, zero_scr, sem):
    i = pl.program_id(0)
    tm = x_ref.shape[0]
    base = i * tm

    def start(k, carry):
        for u in range(ISSUE_UNROLL):
            r = k * ISSUE_UNROLL + u
            _row_copy(x_ref, r, xs_ref, dest_ref[base + r], sem).start(priority=u % 2)
        return carry

    lax.fori_loop(0, tm // ISSUE_UNROLL, start, 0)
    pltpu.make_async_copy(x_ref, xs_ref.at[pl.ds(0, tm)], sem).wait()

    @pl.when(i == pl.num_programs(0) - 1)
    def _():
        zero_scr[...] = jnp.zeros_like(zero_scr)
        for g in range(N_GROUPS + 1):
            first = fill_start_ref[g]

            def start_fill(r, carry):
                _row_copy(zero_scr, 0, xs_ref, first + r, sem).start()
                return carry

            def wait_fill(r, carry):
                _row_copy(zero_scr, 0, xs_ref, 0, sem).wait()
                return carry

            lax.fori_loop(0, fill_n_ref[g], start_fill, 0)
            lax.fori_loop(0, fill_n_ref[g], wait_fill, 0)


def _scatter_call(dest, fill_start, fill_n, x1t, p_rows):
    n = x1t.shape[0]
    tm = TM_PERM
    grid_spec = pltpu.PrefetchScalarGridSpec(
        num_scalar_prefetch=3,
        grid=(n // tm,),
        in_specs=[pl.BlockSpec((tm, SUBLANES, LANES), lambda i, *_: (i, 0, 0))],
        out_specs=pl.BlockSpec(memory_space=pl.ANY),
        scratch_shapes=[pltpu.VMEM((1, SUBLANES, LANES), F32), pltpu.SemaphoreType.DMA(())],
    )
    return pl.pallas_call(
        _scatter_kernel,
        grid_spec=grid_spec,
        out_shape=jax.ShapeDtypeStruct((p_rows, SUBLANES, LANES), F32),
        compiler_params=pltpu.CompilerParams(dimension_semantics=("arbitrary",),
                                             vmem_limit_bytes=VMEM_LIMIT),
        name="row_scatter",
    )(dest, fill_start, fill_n, x1t)


def _moe_kernel(tg_ref, tv_ref, xs_ref, g2_ref, wrh_ref, wrl_ref, br_ref, wg_ref, wu_ref, wd_ref,
                o_ref):
    i = pl.program_id(0)
    t = xs_ref.shape[0]

    @pl.when(tv_ref[i] == 0)
    def _():
        o_ref[...] = jnp.zeros_like(o_ref)

    @pl.when(tv_ref[i] > 0)
    def _():
        x1 = xs_ref[...].reshape(t, D_MODEL)
        h2 = _ffn_norm(x1, g2_ref)

        rowi = lax.broadcasted_iota(jnp.int32, (SUBLANES, t), 0).astype(F32)
        neg = jnp.float32(-jnp.inf)
        logits = _router_logits(wrh_ref, wrl_ref, br_ref, h2)
        gl = jnp.where(rowi < N_GROUPS, logits[0:SUBLANES, :], neg)
        mine = jnp.sum(jnp.where(rowi == tg_ref[i].astype(F32), gl, 0.0), axis=0, keepdims=True)
        g_w = 1.0 / jnp.sum(jnp.exp(gl - mine), axis=0, keepdims=True)
        el = logits[SUBLANES:2 * SUBLANES, :]
        t1, i1 = _first_argmax(el, rowi)
        t2, i2 = _first_argmax(jnp.where(rowi == i1, neg, el), rowi)
        e = jnp.exp(t2 - t1)
        w1 = g_w / (1.0 + e)
        w2 = g_w * e / (1.0 + e)
        cw_t = jnp.where(rowi == i1, w1, 0.0) + jnp.where(rowi == i2, w2, 0.0)
        cw = jnp.concatenate([cw_t, jnp.zeros((LANES - SUBLANES, t), F32)], axis=0).T

        hb = h2.astype(BF16)
        gate_up = lambda ex: (_dot(hb, wg_ref[ex]), _dot(hb, wu_ref[ex]))
        acc = x1
        ahead = gate_up(0)
        for ex in range(E_PER_GROUP):
            gate, up = ahead
            if ex + 1 < E_PER_GROUP:
                ahead = gate_up(ex + 1)
            act = (gate * jax.nn.sigmoid(gate)) * up * cw[:, ex:ex + 1]
            acc = acc + _dot(act.astype(BF16), wd_ref[ex])
        o_ref[...] = acc.reshape(t, SUBLANES, LANES)


def _moe_call(tile_group, tile_valid, xs, g2, wrh, wrl, br, wg, wu, wd):
    p = xs.shape[0]
    t = TM_MOE
    by_block = lambda i, tg, tv: (i, 0, 0)
    by_group = lambda i, tg, tv: (tg[i], 0, 0)
    once = pl.Buffered(1)
    grid_spec = pltpu.PrefetchScalarGridSpec(
        num_scalar_prefetch=2,
        grid=(p // t,),
        in_specs=[pl.BlockSpec((t, SUBLANES, LANES), by_block),
                  pl.BlockSpec((1, D_MODEL), lambda i, *_: (0, 0)),
                  pl.BlockSpec((1, D_MODEL, LANES), by_group, pipeline_mode=once),
                  pl.BlockSpec((1, D_MODEL, LANES), by_group, pipeline_mode=once),
                  pl.BlockSpec((1, 1, LANES), by_group, pipeline_mode=once),
                  pl.BlockSpec((E_PER_GROUP, D_MODEL, D_EXPERT), by_group, pipeline_mode=once),
                  pl.BlockSpec((E_PER_GROUP, D_MODEL, D_EXPERT), by_group, pipeline_mode=once),
                  pl.BlockSpec((E_PER_GROUP, D_EXPERT, D_MODEL), by_group, pipeline_mode=once)],
        out_specs=pl.BlockSpec((t, SUBLANES, LANES), by_block),
    )
    return pl.pallas_call(
        _moe_kernel,
        grid_spec=grid_spec,
        out_shape=jax.ShapeDtypeStruct((p, SUBLANES, LANES), F32),
        compiler_params=pltpu.CompilerParams(dimension_semantics=("arbitrary",),
                                             vmem_limit_bytes=VMEM_LIMIT),
        name="moe_experts",
    )(tile_group, tile_valid, xs, g2, wrh, wrl, br, wg, wu, wd)


def _gather_kernel(dest_ref, ys_ref, o_ref, rows_scr, sem):
    i = pl.program_id(0)
    tm = o_ref.shape[0]
    base = i * tm

    def start(k, carry):
        for u in range(ISSUE_UNROLL):
            r = k * ISSUE_UNROLL + u
            _row_copy(ys_ref, dest_ref[base + r], rows_scr, r, sem).start(priority=u % 2)
        return carry

    lax.fori_loop(0, tm // ISSUE_UNROLL, start, 0)
    pltpu.make_async_copy(ys_ref.at[pl.ds(0, tm)], rows_scr, sem).wait()
    o_ref[...] = rows_scr[...].reshape(tm, D_MODEL)


def _gather_call(dest, ys, n):
    tm = TM_PERM
    grid_spec = pltpu.PrefetchScalarGridSpec(
        num_scalar_prefetch=1,
        grid=(n // tm,),
        in_specs=[pl.BlockSpec(memory_space=pl.ANY)],
        out_specs=pl.BlockSpec((tm, D_MODEL), lambda i, *_: (i, 0)),
        scratch_shapes=[pltpu.VMEM((tm, SUBLANES, LANES), F32), pltpu.SemaphoreType.DMA(())],
    )
    return pl.pallas_call(
        _gather_kernel,
        grid_spec=grid_spec,
        out_shape=jax.ShapeDtypeStruct((n, D_MODEL), F32),
        compiler_params=pltpu.CompilerParams(dimension_semantics=("arbitrary",),
                                             vmem_limit_bytes=VMEM_LIMIT),
        name="row_gather",
    )(dest, ys)


def _router_slabs(w_rg, b_rg, w_re, b_re):
    def slabs(group_part, expert_part):
        lead = expert_part.shape[:-1]
        pads = [jnp.zeros(lead + (width,), F32)
                for width in (SUBLANES - N_GROUPS, LANES - SUBLANES - E_PER_GROUP)]
        return jnp.concatenate([jnp.broadcast_to(group_part, lead + (N_GROUPS,)), pads[0],
                                expert_part, pads[1]], axis=-1)

    w = slabs(w_rg, w_re.reshape(D_MODEL, N_GROUPS, E_PER_GROUP).transpose(1, 0, 2))
    b = slabs(b_rg, b_re.reshape(N_GROUPS, 1, E_PER_GROUP))
    hi = w.astype(BF16)
    lo = (w - hi.astype(F32)).astype(BF16)
    return hi, lo, b


def _layer(x2d, batch, seq_len, norm_mix_g, w_in, q_norm_g, k_norm_g, conv_w, w_sb_branch,
           w_conv_branch, w_out, norm_ffn_g, w_router_group, b_router_group, w_router_expert,
           b_router_expert, w_gate_e, w_up_e, w_down_e):
    n = x2d.shape[0]
    lane = jnp.arange(WIDTH)
    head_mean = jnp.where(lane[:, None] // HEAD_DIM == lane[None, :] // HEAD_DIM,
                          1.0 / HEAD_DIM, 0.0).astype(BF16)
    qg = (jnp.tile(q_norm_g, N_HEADS) * HEAD_DIM ** -0.5)[None, :]
    kg = jnp.tile(k_norm_g, N_HEADS)[None, :]
    qt, k, vt, y, sa, sb = _inproj_call(x2d, norm_mix_g[None, :], w_in.astype(BF16), qg, kg,
                                        head_mean, conv_w, seq_len)

    pos = jnp.arange(TK)
    tri = (pos[None, :] >= pos[:, None]).astype(BF16)
    attn, (wg16, wu16, wd16) = _attn_call(qt, k, vt, tri, (w_gate_e, w_up_e, w_down_e), batch,
                                          seq_len)

    wrh, wrl, br = _router_slabs(w_router_group, b_router_group, w_router_expert, b_router_expert)
    tpos = jnp.arange(TM_PROJ)
    upper = (tpos[:, None] < tpos[None, :]).astype(BF16)
    x1t, route, cnt = _merge_call(attn, y, sa, sb, x2d, w_sb_branch.astype(BF16),
                                  w_conv_branch.astype(BF16), w_out.astype(BF16),
                                  norm_ffn_g[None, :], wrh, wrl, br, upper)

    t = TM_MOE
    grp = route[0].astype(jnp.int32)
    rank = route[1].astype(jnp.int32)
    counts = cnt[0:N_GROUPS, 0].astype(jnp.int32)
    padded = ((counts + t - 1) // t) * t
    ends = jnp.cumsum(padded)
    offs = ends - padded
    dest = rank
    for g in range(N_GROUPS):
        dest = dest + jnp.where(grp == g, offs[g], 0)
    n_tiles = n // t + N_GROUPS
    tile_start = jnp.arange(n_tiles, dtype=jnp.int32) * t
    tile_valid = (tile_start < ends[-1]).astype(jnp.int32)
    tile_group = jnp.zeros_like(tile_start)
    for g in range(N_GROUPS - 1):
        tile_group = tile_group + (jnp.minimum(tile_start, ends[-1] - t) >= ends[g]).astype(jnp.int32)
    p_rows = n_tiles * t
    fill_start = jnp.concatenate([offs + counts, ends[-1:]])
    fill_n = jnp.concatenate([padded - counts, p_rows - ends[-1:]])

    xs = _scatter_call(dest, fill_start, fill_n, x1t, p_rows)
    ys = _moe_call(tile_group, tile_valid, xs, norm_ffn_g[None, :], wrh, wrl, br, wg16, wu16, wd16)
    return _gather_call(dest, ys, n)


def kernel(x, norm_mix_g, w_in, q_norm_g, k_norm_g, conv_w, w_sb_branch, w_conv_branch, w_out,
           norm_ffn_g, w_router_group, b_router_group, w_router_expert, b_router_expert,
           w_gate_e, w_up_e, w_down_e):
    batch, seq_len, d = x.shape
    x2d = x.reshape(batch * seq_len, d)
    for l in range(norm_mix_g.shape[0]):
        x2d = _layer(x2d, batch, seq_len, norm_mix_g[l], w_in[l], q_norm_g[l], k_norm_g[l],
                     conv_w[l], w_sb_branch[l], w_conv_branch[l], w_out[l], norm_ffn_g[l],
                     w_router_group[l], b_router_group[l], w_router_expert[l], b_router_expert[l],
                     w_gate_e[l], w_up_e[l], w_down_e[l])
    return x2d.reshape(batch, seq_len, d)
```

```python
import functools

import jax
import jax.numpy as jnp
from jax import lax
from jax.experimental import pallas as pl
from jax.experimental.pallas import tpu as pltpu

F32 = jnp.float32
BF16 = jnp.bfloat16

D_MODEL = 1024
N_HEADS = 8
HEAD_DIM = 64
WIDTH = N_HEADS * HEAD_DIM
CONV_K = 3
N_GROUPS = 4
E_PER_GROUP = 8
N_EXPERTS = N_GROUPS * E_PER_GROUP
D_EXPERT = 256
EPS = 1e-6

LANES = 128
SUBLANES = 8
HEADS_PER_STEP = LANES // HEAD_DIM
N_PAIRS = N_HEADS // HEADS_PER_STEP

TM_PROJ = 1024
TQ = 256
TK = 256
ROW_BAND = 256
TM_MOE = 512
TM_PERM = 4096
GATHER_CHUNKS = 4
ISSUE_UNROLL = 8
VMEM_LIMIT = 56 * 1024 * 1024

def _dot(a, b):
    return jnp.dot(a, b, preferred_element_type=F32)


def _split_bf16(v):
    hi = v.astype(BF16)
    lo = (v - hi.astype(F32)).astype(BF16)
    return hi, lo


def _inproj_kernel(x_ref, g1_ref, w_ref, qg_ref, kg_ref, pm_ref, cw_ref,
                   qt_ref, k_ref, vt_ref, y_ref, sa_ref, sb_ref, cu_scr, *, tiles_per_seq):
    tm = x_ref.shape[0]

    @pl.when(pl.program_id(0) % tiles_per_seq == 0)
    def _():
        cu_scr[0:SUBLANES, :] = jnp.zeros((SUBLANES, WIDTH), F32)

    bands = [pl.ds(b * ROW_BAND, ROW_BAND) for b in range(tm // ROW_BAND)]
    h_bands = []
    for rows in bands:
        x = x_ref[rows, :]
        ms = jnp.mean(x * x, axis=-1, keepdims=True)
        h_bands.append((x * lax.rsqrt(ms + EPS) * g1_ref[...]).astype(BF16))

    def proj(lo, width):
        return jnp.concatenate([_dot(h, w_ref[:, lo:lo + width]) for h in h_bands], axis=0)

    def head_norm(t, g_ref):
        hms = _dot((t * t).astype(BF16), pm_ref[...])
        return t * lax.rsqrt(hms + EPS) * g_ref[...]

    sa_ref[...] = jax.nn.sigmoid(proj(6 * WIDTH, D_MODEL))
    sb_ref[...] = jax.nn.sigmoid(proj(6 * WIDTH + D_MODEL, D_MODEL))

    c_b = proj(3 * WIDTH, WIDTH)
    cu = proj(4 * WIDTH, WIDTH) * proj(5 * WIDTH, WIDTH)
    cu_scr[SUBLANES:SUBLANES + tm, :] = cu
    cw = cw_ref[...]
    conv = (cw[0:1, :] * cu_scr[SUBLANES - 2:SUBLANES - 2 + tm, :]
            + cw[1:2, :] * cu_scr[SUBLANES - 1:SUBLANES - 1 + tm, :]
            + cw[2:3, :] * cu)
    y_ref[...] = (c_b * conv).astype(BF16)
    cu_scr[0:SUBLANES, :] = cu_scr[tm:tm + SUBLANES, :]

    q = head_norm(proj(0, WIDTH), qg_ref)
    k_ref[...] = head_norm(proj(WIDTH, WIDTH), kg_ref).astype(BF16)
    v = proj(2 * WIDTH, WIDTH)
    for blk in range(tm // TQ):
        qt_ref[blk] = q[blk * TQ:(blk + 1) * TQ, :].T.astype(BF16)
        vt_ref[blk] = v[blk * TK:(blk + 1) * TK, :].T.astype(BF16)


def _inproj_call(x2d, g1, w_in, qg, kg, pm, conv_w, seq_len):
    n = x2d.shape[0]
    tm = TM_PROJ
    in_w = w_in.shape[1]
    const = lambda shape: pl.BlockSpec(shape, lambda i: (0, 0), pipeline_mode=pl.Buffered(1))
    rows = lambda width: pl.BlockSpec((tm, width), lambda i: (i, 0))
    assert TQ == TK and tm % TQ == 0
    cols = pl.BlockSpec((tm // TQ, WIDTH, TQ), lambda i: (i, 0, 0))
    cols_shape = jax.ShapeDtypeStruct((n // TQ, WIDTH, TQ), BF16)
    return pl.pallas_call(
        functools.partial(_inproj_kernel, tiles_per_seq=seq_len // tm),
        grid=(n // tm,),
        in_specs=[rows(D_MODEL), const((1, D_MODEL)), const((D_MODEL, in_w)), const((1, WIDTH)),
                  const((1, WIDTH)), const((WIDTH, WIDTH)), const((CONV_K, WIDTH))],
        out_specs=[cols, rows(WIDTH), cols, rows(WIDTH), rows(D_MODEL), rows(D_MODEL)],
        out_shape=[cols_shape, jax.ShapeDtypeStruct((n, WIDTH), BF16)] * 2
                  + [jax.ShapeDtypeStruct((n, D_MODEL), F32)] * 2,
        scratch_shapes=[pltpu.VMEM((SUBLANES + tm, WIDTH), F32)],
        compiler_params=pltpu.CompilerParams(dimension_semantics=("arbitrary",),
                                             vmem_limit_bytes=VMEM_LIMIT),
        name="inproj",
    )(x2d, g1, w_in, qg, kg, pm, conv_w)


def _softplus(z):
    return jnp.maximum(z, 0.0) + jnp.log(1.0 + jnp.exp(-jnp.abs(z)))


def _attn_kernel(qt_ref, k_ref, vt_ref, tri_ref, wg_ref, wu_ref, wd_ref, o_ref, wg16_ref, wu16_ref,
                 wd16_ref, qm_scr, z_scr, zc_scr, suf_scr, a_scr, acc_scr, c_scr):
    i = pl.program_id(1)
    tq = qt_ref.shape[2]
    feat = lax.broadcasted_iota(jnp.int32, (LANES, tq), 0)
    first_head = feat < HEAD_DIM
    for p in range(N_PAIRS):
        q2 = qt_ref[0, p * LANES:(p + 1) * LANES, :]
        zero = jnp.zeros_like(q2)
        qm_scr[2 * p] = jnp.where(first_head, q2, zero)
        qm_scr[2 * p + 1] = jnp.where(first_head, zero, q2)
    acc_scr[...] = jnp.zeros_like(acc_scr)
    c_scr[...] = jnp.zeros_like(c_scr)

    def pair(s):
        p = s // HEADS_PER_STEP
        return slice(p * LANES, (p + 1) * LANES)

    def scores(s, j):
        keys = k_ref[pl.ds(pl.multiple_of(j * TK, TK), TK), pair(s)]
        z_scr[s] = _dot(keys, qm_scr[s])

    def suffix_sums(s, mask):
        z = z_scr[s]
        zc_scr[s] = z - c_scr[s]
        sp = _softplus(z.astype(BF16))
        if mask is not None:
            sp = jnp.where(mask, sp, jnp.zeros_like(sp))
        suf_scr[s] = _dot(tri_ref[...], sp)

    def weights(s, mask):
        suffix = suf_scr[s]
        a = jnp.exp(zc_scr[s] - suffix)
        if mask is not None:
            a = jnp.where(mask, a, 0.0)
        a_scr[s] = a.astype(BF16)
        c_scr[s] = c_scr[s] + suffix[0:1, :]

    def accumulate(s, j):
        acc_scr[s] = acc_scr[s] + _dot(vt_ref[j, pair(s), :], a_scr[s])

    key = lax.broadcasted_iota(jnp.int32, (TK, tq), 0)
    qry = lax.broadcasted_iota(jnp.int32, (TK, tq), 1)
    causal = key < qry
    nxt = jnp.maximum(i - 1, 0)
    for s in range(N_HEADS):
        scores(s, i)
    for s in range(N_HEADS):
        suffix_sums(s, causal)
        scores(s, nxt)
    for s in range(N_HEADS):
        weights(s, causal)

    def body(t, carry):
        j = i - t
        nxt = jnp.maximum(j - 1, 0)
        for s in range(N_HEADS):
            accumulate(s, j + 1)
            suffix_sums(s, None)
            scores(s, nxt)
        for s in range(N_HEADS):
            weights(s, None)
        return carry

    lax.fori_loop(1, i + 1, body, 0)
    for s in range(N_HEADS):
        accumulate(s, 0)
    for p in range(N_PAIRS):
        both = jnp.where(first_head, acc_scr[2 * p], acc_scr[2 * p + 1])
        o_ref[:, p * LANES:(p + 1) * LANES] = both.T.astype(o_ref.dtype)

    wg16_ref[...] = wg_ref[...].astype(BF16)
    wu16_ref[...] = wu_ref[...].astype(BF16)
    wd16_ref[...] = wd_ref[...].astype(BF16)


def _attn_call(qt, k, vt, tri, expert_weights, batch, seq_len):
    assert TQ == TK
    nq = seq_len // TQ
    steps = batch * nq
    qspec = pl.BlockSpec((1, WIDTH, TQ), lambda b, i: (b * nq + i, 0, 0))
    kspec = pl.BlockSpec((seq_len, WIDTH), lambda b, i: (b, 0))
    vspec = pl.BlockSpec((seq_len // TK, WIDTH, TK), lambda b, i: (b, 0, 0))
    ospec = pl.BlockSpec((TQ, WIDTH), lambda b, i: (b * nq + i, 0))
    slabs = [w.reshape(steps, -1, w.shape[-1]) for w in expert_weights]
    wspecs = [pl.BlockSpec((1,) + s.shape[1:], lambda b, i: (b * nq + i, 0, 0)) for s in slabs]
    attn, *cast = pl.pallas_call(
        _attn_kernel,
        grid=(batch, nq),
        in_specs=[qspec, kspec, vspec, pl.BlockSpec((TK, TK), lambda b, i: (0, 0))] + wspecs,
        out_specs=[ospec] + wspecs,
        out_shape=[jax.ShapeDtypeStruct(k.shape, BF16)]
                  + [jax.ShapeDtypeStruct(s.shape, BF16) for s in slabs],
        scratch_shapes=[pltpu.VMEM((N_HEADS, LANES, TQ), BF16),
                        pltpu.VMEM((N_HEADS, TK, TQ), F32),
                        pltpu.VMEM((N_HEADS, TK, TQ), F32),
                        pltpu.VMEM((N_HEADS, TK, TQ), F32),
                        pltpu.VMEM((N_HEADS, TK, TQ), BF16),
                        pltpu.VMEM((N_HEADS, LANES, TQ), F32),
                        pltpu.VMEM((N_HEADS, 1, TQ), F32)],
        compiler_params=pltpu.CompilerParams(dimension_semantics=("arbitrary", "arbitrary"),
                                             vmem_limit_bytes=VMEM_LIMIT),
        name="sb_attn",
    )(qt, k, vt, tri, *slabs)
    return attn, [c.reshape(w.shape) for c, w in zip(cast, expert_weights)]


def _router_logits(w_hi_ref, w_lo_ref, b_ref, h):
    hh, hl = _split_bf16(h)
    logits = _dot(hh, w_hi_ref[0]) + _dot(hl, w_hi_ref[0]) + _dot(hh, w_lo_ref[0]) + b_ref[0]
    return logits.T


def _first_argmax(vals, rowi):
    top = jnp.max(vals, axis=0, keepdims=True)
    idx = jnp.min(jnp.where(vals == top, rowi, jnp.float32(SUBLANES)), axis=0, keepdims=True)
    return top, idx


def _ffn_norm(x1, g2_ref):
    ms = jnp.mean(x1 * x1, axis=-1, keepdims=True)
    return x1 * lax.rsqrt(ms + EPS) * g2_ref[...]


def _merge_kernel(attn_ref, y_ref, sa_ref, sb_ref, x_ref, wsb_ref, wcv_ref, wout_ref, g2_ref,
                  wrh_ref, wrl_ref, br_ref, upper_ref, x1t_ref, route_ref, cnt_ref, cnt_scr):
    tm = x_ref.shape[0]

    @pl.when(pl.program_id(0) == 0)
    def _():
        cnt_scr[...] = jnp.zeros_like(cnt_scr)

    bands = [pl.ds(b * ROW_BAND, ROW_BAND) for b in range(tm // ROW_BAND)]
    merged = [(sa_ref[rows, :] * _dot(attn_ref[rows, :], wsb_ref[...])
               + sb_ref[rows, :] * _dot(y_ref[rows, :], wcv_ref[...])).astype(BF16)
              for rows in bands]
    x1 = [x_ref[rows, :] + _dot(m, wout_ref[...]) for rows, m in zip(bands, merged)]
    logits = []
    for rows, x1_band in zip(bands, x1):
        x1t_ref[rows] = x1_band.reshape(ROW_BAND, SUBLANES, LANES)
        logits.append(_router_logits(wrh_ref, wrl_ref, br_ref, _ffn_norm(x1_band, g2_ref)))

    rowi = lax.broadcasted_iota(jnp.int32, (SUBLANES, tm), 0).astype(F32)
    gl = jnp.concatenate([lg[0:SUBLANES, :] for lg in logits], axis=1)
    gl = jnp.where(rowi < N_GROUPS, gl, jnp.float32(-jnp.inf))
    _, gidx = _first_argmax(gl, rowi)

    onehot = jnp.where(rowi == gidx, 1.0, 0.0)
    before = _dot(onehot.astype(BF16), upper_ref[...]) + cnt_scr[:, 0:1]
    rank = jnp.sum(onehot * before, axis=0, keepdims=True)
    cnt_scr[...] = cnt_scr[...] + jnp.sum(onehot, axis=1, keepdims=True)
    cnt_ref[...] = cnt_scr[...]
    route_ref[...] = jnp.where(rowi == 0, gidx, jnp.where(rowi == 1, rank, 0.0))


def _merge_call(attn, y, sa, sb, x2d, wsb, wcv, wout, g2, wrh, wrl, br, upper):
    n = x2d.shape[0]
    tm = TM_PROJ
    rows = lambda width: pl.BlockSpec((tm, width), lambda i: (i, 0))
    const = lambda shape: pl.BlockSpec(shape, lambda i: (0,) * len(shape),
                                       pipeline_mode=pl.Buffered(1))
    slab = (1, D_MODEL, LANES)
    return pl.pallas_call(
        _merge_kernel,
        grid=(n // tm,),
        in_specs=[rows(WIDTH), rows(WIDTH), rows(D_MODEL), rows(D_MODEL), rows(D_MODEL),
                  const((WIDTH, D_MODEL)), const((WIDTH, D_MODEL)), const((D_MODEL, D_MODEL)),
                  const((1, D_MODEL)), const(slab), const(slab), const((1, 1, LANES)),
                  const((tm, tm))],
        out_specs=[pl.BlockSpec((tm, SUBLANES, LANES), lambda i: (i, 0, 0)),
                   pl.BlockSpec((SUBLANES, tm), lambda i: (0, i)),
                   pl.BlockSpec((SUBLANES, LANES), lambda i: (0, 0))],
        out_shape=[jax.ShapeDtypeStruct((n, SUBLANES, LANES), F32),
                   jax.ShapeDtypeStruct((SUBLANES, n), F32),
                   jax.ShapeDtypeStruct((SUBLANES, LANES), F32)],
        scratch_shapes=[pltpu.VMEM((SUBLANES, LANES), F32)],
        compiler_params=pltpu.CompilerParams(dimension_semantics=("arbitrary",),
                                             vmem_limit_bytes=VMEM_LIMIT),
        name="merge_router",
    )(attn, y, sa, sb, x2d, wsb, wcv, wout, g2, wrh, wrl, br, upper)


def _row_copy(src_ref, src_row, dst_ref, dst_row, sem):
    return pltpu.make_async_copy(src_ref.at[src_row], dst_ref.at[dst_row], sem)


def _scatter_kernel(dest_ref, fill_start_ref, fill_n_ref, x_ref, xs_ref, zero_scr, sem):
    i = pl.program_id(0)
    tm = x_ref.shape[0]
    base = i * tm

    def start(k, carry):
        for u in range(ISSUE_UNROLL):
            r = k * ISSUE_UNROLL + u
            _row_copy(x_ref, r, xs_ref, dest_ref[base + r], sem).start(priority=u % 2)
        return carry

    lax.fori_loop(0, tm // ISSUE_UNROLL, start, 0)
    pltpu.make_async_copy(x_ref, xs_ref.at[pl.ds(0, tm)], sem).wait()

    @pl.when(i == pl.num_programs(0) - 1)
    def _():
        zero_scr[...] = jnp.zeros_like(zero_scr)
        for g in range(N_GROUPS + 1):
            first = fill_start_ref[g]

            def start_fill(r, carry):
                _row_copy(zero_scr, 0, xs_ref, first + r, sem).start()
                return carry

            def wait_fill(r, carry):
                _row_copy(zero_scr, 0, xs_ref, 0, sem).wait()
                return carry

            lax.fori_loop(0, fill_n_ref[g], start_fill, 0)
            lax.fori_loop(0, fill_n_ref[g], wait_fill, 0)


def _scatter_call(dest, fill_start, fill_n, x1t, p_rows):
    n = x1t.shape[0]
    tm = TM_PERM
    grid_spec = pltpu.PrefetchScalarGridSpec(
        num_scalar_prefetch=3,
        grid=(n // tm,),
        in_specs=[pl.BlockSpec((tm, SUBLANES, LANES), lambda i, *_: (i, 0, 0))],
        out_specs=pl.BlockSpec(memory_space=pl.ANY),
        scratch_shapes=[pltpu.VMEM((1, SUBLANES, LANES), F32), pltpu.SemaphoreType.DMA(())],
    )
    return pl.pallas_call(
        _scatter_kernel,
        grid_spec=grid_spec,
        out_shape=jax.ShapeDtypeStruct((p_rows, SUBLANES, LANES), F32),
        compiler_params=pltpu.CompilerParams(dimension_semantics=("arbitrary",),
                                             vmem_limit_bytes=VMEM_LIMIT),
        name="row_scatter",
    )(dest, fill_start, fill_n, x1t)


def _moe_kernel(tg_ref, tv_ref, xs_ref, g2_ref, wrh_ref, wrl_ref, br_ref, wg_ref, wu_ref, wd_ref,
                o_ref):
    i = pl.program_id(0)
    t = xs_ref.shape[0]

    @pl.when(tv_ref[i] == 0)
    def _():
        o_ref[...] = jnp.zeros_like(o_ref)

    @pl.when(tv_ref[i] > 0)
    def _():
        x1 = xs_ref[...].reshape(t, D_MODEL)
        h2 = _ffn_norm(x1, g2_ref)

        rowi = lax.broadcasted_iota(jnp.int32, (SUBLANES, t), 0).astype(F32)
        neg = jnp.float32(-jnp.inf)
        logits = _router_logits(wrh_ref, wrl_ref, br_ref, h2)
        gl = jnp.where(rowi < N_GROUPS, logits[0:SUBLANES, :], neg)
        mine = jnp.sum(jnp.where(rowi == tg_ref[i].astype(F32), gl, 0.0), axis=0, keepdims=True)
        g_w = 1.0 / jnp.sum(jnp.exp(gl - mine), axis=0, keepdims=True)
        el = logits[SUBLANES:2 * SUBLANES, :]
        t1, i1 = _first_argmax(el, rowi)
        t2, i2 = _first_argmax(jnp.where(rowi == i1, neg, el), rowi)
        e = jnp.exp(t2 - t1)
        w1 = g_w / (1.0 + e)
        w2 = g_w * e / (1.0 + e)
        cw_t = jnp.where(rowi == i1, w1, 0.0) + jnp.where(rowi == i2, w2, 0.0)
        cw = jnp.concatenate([cw_t, jnp.zeros((LANES - SUBLANES, t), F32)], axis=0).T

        hb = h2.astype(BF16)
        gate_up = lambda ex: (_dot(hb, wg_ref[ex]), _dot(hb, wu_ref[ex]))
        acc = x1
        ahead = gate_up(0)
        for ex in range(E_PER_GROUP):
            gate, up = ahead
            if ex + 1 < E_PER_GROUP:
                ahead = gate_up(ex + 1)
            act = (gate * jax.nn.sigmoid(gate)) * up * cw[:, ex:ex + 1]
            acc = acc + _dot(act.astype(BF16), wd_ref[ex])
        o_ref[...] = acc.reshape(t, SUBLANES, LANES)


def _moe_call(tile_group, tile_valid, xs, g2, wrh, wrl, br, wg, wu, wd):
    p = xs.shape[0]
    t = TM_MOE
    by_block = lambda i, tg, tv: (i, 0, 0)
    by_group = lambda i, tg, tv: (tg[i], 0, 0)
    once = pl.Buffered(1)
    grid_spec = pltpu.PrefetchScalarGridSpec(
        num_scalar_prefetch=2,
        grid=(p // t,),
        in_specs=[pl.BlockSpec((t, SUBLANES, LANES), by_block),
                  pl.BlockSpec((1, D_MODEL), lambda i, *_: (0, 0)),
                  pl.BlockSpec((1, D_MODEL, LANES), by_group, pipeline_mode=once),
                  pl.BlockSpec((1, D_MODEL, LANES), by_group, pipeline_mode=once),
                  pl.BlockSpec((1, 1, LANES), by_group, pipeline_mode=once),
                  pl.BlockSpec((E_PER_GROUP, D_MODEL, D_EXPERT), by_group, pipeline_mode=once),
                  pl.BlockSpec((E_PER_GROUP, D_MODEL, D_EXPERT), by_group, pipeline_mode=once),
                  pl.BlockSpec((E_PER_GROUP, D_EXPERT, D_MODEL), by_group, pipeline_mode=once)],
        out_specs=pl.BlockSpec((t, SUBLANES, LANES), by_block),
    )
    return pl.pallas_call(
        _moe_kernel,
        grid_spec=grid_spec,
        out_shape=jax.ShapeDtypeStruct((p, SUBLANES, LANES), F32),
        compiler_params=pltpu.CompilerParams(dimension_semantics=("arbitrary",),
                                             vmem_limit_bytes=VMEM_LIMIT),
        name="moe_experts",
    )(tile_group, tile_valid, xs, g2, wrh, wrl, br, wg, wu, wd)


def _gather_kernel(dest_ref, ys_ref, o_ref, rows_scr, sems):
    i = pl.program_id(0)
    tm = o_ref.shape[0]
    base = i * tm
    chunk = tm // GATHER_CHUNKS

    for c in range(GATHER_CHUNKS):
        def start(k, carry, c=c):
            for u in range(ISSUE_UNROLL):
                r = c * chunk + k * ISSUE_UNROLL + u
                _row_copy(ys_ref, dest_ref[base + r], rows_scr, r, sems.at[c]).start(priority=u % 2)
            return carry

        lax.fori_loop(0, chunk // ISSUE_UNROLL, start, 0)

    for c in range(GATHER_CHUNKS):
        rows = pl.ds(c * chunk, chunk)
        pltpu.make_async_copy(ys_ref.at[pl.ds(0, chunk)], rows_scr.at[rows], sems.at[c]).wait()
        o_ref[rows, :] = rows_scr[rows].reshape(chunk, D_MODEL)


def _gather_call(dest, ys, n):
    tm = TM_PERM
    grid_spec = pltpu.PrefetchScalarGridSpec(
        num_scalar_prefetch=1,
        grid=(n // tm,),
        in_specs=[pl.BlockSpec(memory_space=pl.ANY)],
        out_specs=pl.BlockSpec((tm, D_MODEL), lambda i, *_: (i, 0)),
        scratch_shapes=[pltpu.VMEM((tm, SUBLANES, LANES), F32),
                        pltpu.SemaphoreType.DMA((GATHER_CHUNKS,))],
    )
    return pl.pallas_call(
        _gather_kernel,
        grid_spec=grid_spec,
        out_shape=jax.ShapeDtypeStruct((n, D_MODEL), F32),
        compiler_params=pltpu.CompilerParams(dimension_semantics=("arbitrary",),
                                             vmem_limit_bytes=VMEM_LIMIT),
        name="row_gather",
    )(dest, ys)


def _router_slabs(w_rg, b_rg, w_re, b_re):
    def slabs(group_part, expert_part):
        lead = expert_part.shape[:-1]
        pads = [jnp.zeros(lead + (width,), F32)
                for width in (SUBLANES - N_GROUPS, LANES - SUBLANES - E_PER_GROUP)]
        return jnp.concatenate([jnp.broadcast_to(group_part, lead + (N_GROUPS,)), pads[0],
                                expert_part, pads[1]], axis=-1)

    w = slabs(w_rg, w_re.reshape(D_MODEL, N_GROUPS, E_PER_GROUP).transpose(1, 0, 2))
    b = slabs(b_rg, b_re.reshape(N_GROUPS, 1, E_PER_GROUP))
    hi = w.astype(BF16)
    lo = (w - hi.astype(F32)).astype(BF16)
    return hi, lo, b


def _layer(x2d, batch, seq_len, norm_mix_g, w_in, q_norm_g, k_norm_g, conv_w, w_sb_branch,
           w_conv_branch, w_out, norm_ffn_g, w_router_group, b_router_group, w_router_expert,
           b_router_expert, w_gate_e, w_up_e, w_down_e):
    n = x2d.shape[0]
    lane = jnp.arange(WIDTH)
    head_mean = jnp.where(lane[:, None] // HEAD_DIM == lane[None, :] // HEAD_DIM,
                          1.0 / HEAD_DIM, 0.0).astype(BF16)
    qg = (jnp.tile(q_norm_g, N_HEADS) * HEAD_DIM ** -0.5)[None, :]
    kg = jnp.tile(k_norm_g, N_HEADS)[None, :]
    qt, k, vt, y, sa, sb = _inproj_call(x2d, norm_mix_g[None, :], w_in.astype(BF16), qg, kg,
                                        head_mean, conv_w, seq_len)

    pos = jnp.arange(TK)
    tri = (pos[None, :] >= pos[:, None]).astype(BF16)
    attn, (wg16, wu16, wd16) = _attn_call(qt, k, vt, tri, (w_gate_e, w_up_e, w_down_e), batch,
                                          seq_len)

    wrh, wrl, br = _router_slabs(w_router_group, b_router_group, w_router_expert, b_router_expert)
    tpos = jnp.arange(TM_PROJ)
    upper = (tpos[:, None] < tpos[None, :]).astype(BF16)
    x1t, route, cnt = _merge_call(attn, y, sa, sb, x2d, w_sb_branch.astype(BF16),
                                  w_conv_branch.astype(BF16), w_out.astype(BF16),
                                  norm_ffn_g[None, :], wrh, wrl, br, upper)

    t = TM_MOE
    grp = route[0].astype(jnp.int32)
    rank = route[1].astype(jnp.int32)
    counts = cnt[0:N_GROUPS, 0].astype(jnp.int32)
    padded = ((counts + t - 1) // t) * t
    ends = jnp.cumsum(padded)
    offs = ends - padded
    dest = rank
    for g in range(N_GROUPS):
        dest = dest + jnp.where(grp == g, offs[g], 0)
    n_tiles = n // t + N_GROUPS
    tile_start = jnp.arange(n_tiles, dtype=jnp.int32) * t
    tile_valid = (tile_start < ends[-1]).astype(jnp.int32)
    tile_group = jnp.zeros_like(tile_start)
    for g in range(N_GROUPS - 1):
        tile_group = tile_group + (jnp.minimum(tile_start, ends[-1] - t) >= ends[g]).astype(jnp.int32)
    p_rows = n_tiles * t
    fill_start = jnp.concatenate([offs + counts, ends[-1:]])
    fill_n = jnp.concatenate([padded - counts, p_rows - ends[-1:]])

    xs = _scatter_call(dest, fill_start, fill_n, x1t, p_rows)
    ys = _moe_call(tile_group, tile_valid, xs, norm_ffn_g[None, :], wrh, wrl, br, wg16, wu16, wd16)
    return _gather_call(dest, ys, n)


def kernel(x, norm_mix_g, w_in, q_norm_g, k_norm_g, conv_w, w_sb_branch, w_conv_branch, w_out,
           norm_ffn_g, w_router_group, b_router_group, w_router_expert, b_router_expert,
           w_gate_e, w_up_e, w_down_e):
    batch, seq_len, d = x.shape
    x2d = x.reshape(batch * seq_len, d)
    for l in range(norm_mix_g.shape[0]):
        x2d = _layer(x2d, batch, seq_len, norm_mix_g[l], w_in[l], q_norm_g[l], k_norm_g[l],
                     conv_w[l], w_sb_branch[l], w_conv_branch[l], w_out[l], norm_ffn_g[l],
                     w_router_group[l], b_router_group[l], w_router_expert[l], b_router_expert[l],
                     w_gate_e[l], w_up_e[l], w_down_e[l])
    return x2d.reshape(batch, seq_len, d)
```

```python
import functools

import jax
import jax.numpy as jnp
from jax import lax
from jax.experimental import pallas as pl
from jax.experimental.pallas import tpu as pltpu

F32 = jnp.float32
BF16 = jnp.bfloat16

D_MODEL = 1024
N_HEADS = 8
HEAD_DIM = 64
WIDTH = N_HEADS * HEAD_DIM
CONV_K = 3
N_GROUPS = 4
E_PER_GROUP = 8
N_EXPERTS = N_GROUPS * E_PER_GROUP
D_EXPERT = 256
EPS = 1e-6

LANES = 128
SUBLANES = 8
HEADS_PER_STEP = LANES // HEAD_DIM
N_PAIRS = N_HEADS // HEADS_PER_STEP

TM_PROJ = 1024
TQ = 256
TK = 256
ROW_BAND = 256
TM_MOE = 512
TM_PERM = 4096
ISSUE_UNROLL = 8
VMEM_LIMIT = 56 * 1024 * 1024

def _dot(a, b):
    return jnp.dot(a, b, preferred_element_type=F32)


def _split_bf16(v):
    hi = v.astype(BF16)
    lo = (v - hi.astype(F32)).astype(BF16)
    return hi, lo


def _inproj_kernel(x_ref, g1_ref, w_ref, qg_ref, kg_ref, pm_ref, cw_ref,
                   qt_ref, k_ref, vt_ref, y_ref, sa_ref, sb_ref, cu_scr, *, tiles_per_seq):
    tm = x_ref.shape[0]

    @pl.when(pl.program_id(0) % tiles_per_seq == 0)
    def _():
        cu_scr[0:SUBLANES, :] = jnp.zeros((SUBLANES, WIDTH), F32)

    bands = [pl.ds(b * ROW_BAND, ROW_BAND) for b in range(tm // ROW_BAND)]
    h_bands = []
    for rows in bands:
        x = x_ref[rows, :]
        ms = jnp.mean(x * x, axis=-1, keepdims=True)
        h_bands.append((x * lax.rsqrt(ms + EPS) * g1_ref[...]).astype(BF16))

    def proj(lo, width):
        return jnp.concatenate([_dot(h, w_ref[:, lo:lo + width]) for h in h_bands], axis=0)

    def head_norm(t, g_ref):
        hms = _dot((t * t).astype(BF16), pm_ref[...])
        return t * lax.rsqrt(hms + EPS) * g_ref[...]

    sa_ref[...] = jax.nn.sigmoid(proj(6 * WIDTH, D_MODEL))
    sb_ref[...] = jax.nn.sigmoid(proj(6 * WIDTH + D_MODEL, D_MODEL))

    c_b = proj(3 * WIDTH, WIDTH)
    cu = proj(4 * WIDTH, WIDTH) * proj(5 * WIDTH, WIDTH)
    cu_scr[SUBLANES:SUBLANES + tm, :] = cu
    cw = cw_ref[...]
    conv = (cw[0:1, :] * cu_scr[SUBLANES - 2:SUBLANES - 2 + tm, :]
            + cw[1:2, :] * cu_scr[SUBLANES - 1:SUBLANES - 1 + tm, :]
            + cw[2:3, :] * cu)
    y_ref[...] = (c_b * conv).astype(BF16)
    cu_scr[0:SUBLANES, :] = cu_scr[tm:tm + SUBLANES, :]

    q = head_norm(proj(0, WIDTH), qg_ref)
    k_ref[...] = head_norm(proj(WIDTH, WIDTH), kg_ref).astype(BF16)
    v = proj(2 * WIDTH, WIDTH)
    for blk in range(tm // TQ):
        qt_ref[blk] = q[blk * TQ:(blk + 1) * TQ, :].T.astype(BF16)
        vt_ref[blk] = v[blk * TK:(blk + 1) * TK, :].T.astype(BF16)


def _inproj_call(x2d, g1, w_in, qg, kg, pm, conv_w, seq_len):
    n = x2d.shape[0]
    tm = TM_PROJ
    in_w = w_in.shape[1]
    const = lambda shape: pl.BlockSpec(shape, lambda i: (0, 0), pipeline_mode=pl.Buffered(1))
    rows = lambda width: pl.BlockSpec((tm, width), lambda i: (i, 0))
    assert TQ == TK and tm % TQ == 0
    cols = pl.BlockSpec((tm // TQ, WIDTH, TQ), lambda i: (i, 0, 0))
    cols_shape = jax.ShapeDtypeStruct((n // TQ, WIDTH, TQ), BF16)
    return pl.pallas_call(
        functools.partial(_inproj_kernel, tiles_per_seq=seq_len // tm),
        grid=(n // tm,),
        in_specs=[rows(D_MODEL), const((1, D_MODEL)), const((D_MODEL, in_w)), const((1, WIDTH)),
                  const((1, WIDTH)), const((WIDTH, WIDTH)), const((CONV_K, WIDTH))],
        out_specs=[cols, rows(WIDTH), cols, rows(WIDTH), rows(D_MODEL), rows(D_MODEL)],
        out_shape=[cols_shape, jax.ShapeDtypeStruct((n, WIDTH), BF16)] * 2
                  + [jax.ShapeDtypeStruct((n, D_MODEL), F32)] * 2,
        scratch_shapes=[pltpu.VMEM((SUBLANES + tm, WIDTH), F32)],
        compiler_params=pltpu.CompilerParams(dimension_semantics=("arbitrary",),
                                             vmem_limit_bytes=VMEM_LIMIT),
        name="inproj",
    )(x2d, g1, w_in, qg, kg, pm, conv_w)


def _softplus(z):
    return jnp.maximum(z, 0.0) + jnp.log(1.0 + jnp.exp(-jnp.abs(z)))


def _attn_kernel(qt_ref, k_ref, vt_ref, tri_ref, wg_ref, wu_ref, wd_ref, o_ref, wg16_ref, wu16_ref,
                 wd16_ref, qm_scr, z_scr, zc_scr, suf_scr, a_scr, acc_scr, c_scr):
    i = pl.program_id(1)
    tq = qt_ref.shape[2]
    feat = lax.broadcasted_iota(jnp.int32, (LANES, tq), 0)
    first_head = feat < HEAD_DIM
    for p in range(N_PAIRS):
        q2 = qt_ref[0, p * LANES:(p + 1) * LANES, :]
        zero = jnp.zeros_like(q2)
        qm_scr[2 * p] = jnp.where(first_head, q2, zero)
        qm_scr[2 * p + 1] = jnp.where(first_head, zero, q2)
    acc_scr[...] = jnp.zeros_like(acc_scr)
    c_scr[...] = jnp.zeros_like(c_scr)

    def pair(s):
        p = s // HEADS_PER_STEP
        return slice(p * LANES, (p + 1) * LANES)

    def scores(s, j):
        keys = k_ref[pl.ds(pl.multiple_of(j * TK, TK), TK), pair(s)]
        z_scr[s] = _dot(keys, qm_scr[s])

    def suffix_sums(s, mask):
        z = z_scr[s]
        zc_scr[s] = z - c_scr[s]
        sp = _softplus(z.astype(BF16))
        if mask is not None:
            sp = jnp.where(mask, sp, jnp.zeros_like(sp))
        suf_scr[s] = _dot(tri_ref[...], sp)

    def weights(s, mask):
        suffix = suf_scr[s]
        a = jnp.exp(zc_scr[s] - suffix)
        if mask is not None:
            a = jnp.where(mask, a, 0.0)
        a_scr[s] = a.astype(BF16)
        c_scr[s] = c_scr[s] + suffix[0:1, :]

    def accumulate(s, j):
        acc_scr[s] = acc_scr[s] + _dot(vt_ref[j, pair(s), :], a_scr[s])

    key = lax.broadcasted_iota(jnp.int32, (TK, tq), 0)
    qry = lax.broadcasted_iota(jnp.int32, (TK, tq), 1)
    causal = key < qry
    nxt = jnp.maximum(i - 1, 0)
    for s in range(N_HEADS):
        scores(s, i)
    for s in range(N_HEADS):
        suffix_sums(s, causal)
        scores(s, nxt)
    for s in range(N_HEADS):
        weights(s, causal)

    def body(t, carry):
        j = i - t
        nxt = jnp.maximum(j - 1, 0)
        for s in range(N_HEADS):
            accumulate(s, j + 1)
            suffix_sums(s, None)
            scores(s, nxt)
        for s in range(N_HEADS):
            weights(s, None)
        return carry

    lax.fori_loop(1, i + 1, body, 0)
    for s in range(N_HEADS):
        accumulate(s, 0)
    for p in range(N_PAIRS):
        both = jnp.where(first_head, acc_scr[2 * p], acc_scr[2 * p + 1])
        o_ref[:, p * LANES:(p + 1) * LANES] = both.T.astype(o_ref.dtype)

    wg16_ref[...] = wg_ref[...].astype(BF16)
    wu16_ref[...] = wu_ref[...].astype(BF16)
    wd16_ref[...] = wd_ref[...].astype(BF16)


def _attn_call(qt, k, vt, tri, expert_weights, batch, seq_len):
    assert TQ == TK
    nq = seq_len // TQ
    steps = batch * nq
    qspec = pl.BlockSpec((1, WIDTH, TQ), lambda b, i: (b * nq + i, 0, 0))
    kspec = pl.BlockSpec((seq_len, WIDTH), lambda b, i: (b, 0))
    vspec = pl.BlockSpec((seq_len // TK, WIDTH, TK), lambda b, i: (b, 0, 0))
    ospec = pl.BlockSpec((TQ, WIDTH), lambda b, i: (b * nq + i, 0))
    slabs = [w.reshape(steps, -1, w.shape[-1]) for w in expert_weights]
    wspecs = [pl.BlockSpec((1,) + s.shape[1:], lambda b, i: (b * nq + i, 0, 0)) for s in slabs]
    attn, *cast = pl.pallas_call(
        _attn_kernel,
        grid=(batch, nq),
        in_specs=[qspec, kspec, vspec, pl.BlockSpec((TK, TK), lambda b, i: (0, 0))] + wspecs,
        out_specs=[ospec] + wspecs,
        out_shape=[jax.ShapeDtypeStruct(k.shape, BF16)]
                  + [jax.ShapeDtypeStruct(s.shape, BF16) for s in slabs],
        scratch_shapes=[pltpu.VMEM((N_HEADS, LANES, TQ), BF16),
                        pltpu.VMEM((N_HEADS, TK, TQ), F32),
                        pltpu.VMEM((N_HEADS, TK, TQ), F32),
                        pltpu.VMEM((N_HEADS, TK, TQ), F32),
                        pltpu.VMEM((N_HEADS, TK, TQ), BF16),
                        pltpu.VMEM((N_HEADS, LANES, TQ), F32),
                        pltpu.VMEM((N_HEADS, 1, TQ), F32)],
        compiler_params=pltpu.CompilerParams(dimension_semantics=("arbitrary", "arbitrary"),
                                             vmem_limit_bytes=VMEM_LIMIT),
        name="sb_attn",
    )(qt, k, vt, tri, *slabs)
    return attn, [c.reshape(w.shape) for c, w in zip(cast, expert_weights)]


def _router_logits(w_hi_ref, w_lo_ref, b_ref, h):
    hh, hl = _split_bf16(h)
    logits = _dot(hh, w_hi_ref[0]) + _dot(hl, w_hi_ref[0]) + _dot(hh, w_lo_ref[0]) + b_ref[0]
    return logits.T


def _first_argmax(vals, rowi):
    top = jnp.max(vals, axis=0, keepdims=True)
    idx = jnp.min(jnp.where(vals == top, rowi, jnp.float32(SUBLANES)), axis=0, keepdims=True)
    return top, idx


def _ffn_norm(x1, g2_ref):
    ms = jnp.mean(x1 * x1, axis=-1, keepdims=True)
    return x1 * lax.rsqrt(ms + EPS) * g2_ref[...]


def _merge_kernel(attn_ref, y_ref, sa_ref, sb_ref, x_ref, wsb_ref, wcv_ref, wout_ref, g2_ref,
                  wrh_ref, wrl_ref, br_ref, upper_ref, x1t_ref, route_ref, cnt_ref, cnt_scr):
    tm = x_ref.shape[0]

    @pl.when(pl.program_id(0) == 0)
    def _():
        cnt_scr[...] = jnp.zeros_like(cnt_scr)

    bands = [pl.ds(b * ROW_BAND, ROW_BAND) for b in range(tm // ROW_BAND)]
    merged = [(sa_ref[rows, :] * _dot(attn_ref[rows, :], wsb_ref[...])
               + sb_ref[rows, :] * _dot(y_ref[rows, :], wcv_ref[...])).astype(BF16)
              for rows in bands]
    x1 = [x_ref[rows, :] + _dot(m, wout_ref[...]) for rows, m in zip(bands, merged)]
    logits = []
    for rows, x1_band in zip(bands, x1):
        x1t_ref[rows] = x1_band.reshape(ROW_BAND, SUBLANES, LANES)
        logits.append(_router_logits(wrh_ref, wrl_ref, br_ref, _ffn_norm(x1_band, g2_ref)))

    rowi = lax.broadcasted_iota(jnp.int32, (SUBLANES, tm), 0).astype(F32)
    gl = jnp.concatenate([lg[0:SUBLANES, :] for lg in logits], axis=1)
    gl = jnp.where(rowi < N_GROUPS, gl, jnp.float32(-jnp.inf))
    _, gidx = _first_argmax(gl, rowi)

    onehot = jnp.where(rowi == gidx, 1.0, 0.0)
    before = _dot(onehot.astype(BF16), upper_ref[...]) + cnt_scr[:, 0:1]
    rank = jnp.sum(onehot * before, axis=0, keepdims=True)
    cnt_scr[...] = cnt_scr[...] + jnp.sum(onehot, axis=1, keepdims=True)
    cnt_ref[...] = cnt_scr[...]
    route_ref[...] = jnp.where(rowi == 0, gidx, jnp.where(rowi == 1, rank, 0.0))


def _merge_call(attn, y, sa, sb, x2d, wsb, wcv, wout, g2, wrh, wrl, br, upper):
    n = x2d.shape[0]
    tm = TM_PROJ
    rows = lambda width: pl.BlockSpec((tm, width), lambda i: (i, 0))
    const = lambda shape: pl.BlockSpec(shape, lambda i: (0,) * len(shape),
                                       pipeline_mode=pl.Buffered(1))
    slab = (1, D_MODEL, LANES)
    return pl.pallas_call(
        _merge_kernel,
        grid=(n // tm,),
        in_specs=[rows(WIDTH), rows(WIDTH), rows(D_MODEL), rows(D_MODEL), rows(D_MODEL),
                  const((WIDTH, D_MODEL)), const((WIDTH, D_MODEL)), const((D_MODEL, D_MODEL)),
                  const((1, D_MODEL)), const(slab), const(slab), const((1, 1, LANES)),
                  const((tm, tm))],
        out_specs=[pl.BlockSpec((tm, SUBLANES, LANES), lambda i: (i, 0, 0)),
                   pl.BlockSpec((SUBLANES, tm), lambda i: (0, i)),
                   pl.BlockSpec((SUBLANES, LANES), lambda i: (0, 0))],
        out_shape=[jax.ShapeDtypeStruct((n, SUBLANES, LANES), F32),
                   jax.ShapeDtypeStruct((SUBLANES, n), F32),
                   jax.ShapeDtypeStruct((SUBLANES, LANES), F32)],
        scratch_shapes=[pltpu.VMEM((SUBLANES, LANES), F32)],
        compiler_params=pltpu.CompilerParams(dimension_semantics=("arbitrary",),
                                             vmem_limit_bytes=VMEM_LIMIT),
        name="merge_router",
    )(attn, y, sa, sb, x2d, wsb, wcv, wout, g2, wrh, wrl, br, upper)


def _row_copy(src_ref, src_row, dst_ref, dst_row, sem):
    return pltpu.make_async_copy(src_ref.at[src_row], dst_ref.at[dst_row], sem)


def _scatter_kernel(dest_ref, fill_start_ref, fill_n_ref, x_ref, xs_ref, zero_scr, sem):
    i = pl.program_id(0)
    tm = x_ref.shape[0]
    base = i * tm

    def start(k, carry):
        for u in range(ISSUE_UNROLL):
            r = k * ISSUE_UNROLL + u
            _row_copy(x_ref, r, xs_ref, dest_ref[base + r], sem).start(priority=u % 2)
        return carry

    lax.fori_loop(0, tm // ISSUE_UNROLL, start, 0)
    pltpu.make_async_copy(x_ref, xs_ref.at[pl.ds(0, tm)], sem).wait()

    @pl.when(i == pl.num_programs(0) - 1)
    def _():
        zero_scr[...] = jnp.zeros_like(zero_scr)
        for g in range(N_GROUPS + 1):
            first = fill_start_ref[g]

            def start_fill(r, carry):
                _row_copy(zero_scr, 0, xs_ref, first + r, sem).start()
                return carry

            def wait_fill(r, carry):
                _row_copy(zero_scr, 0, xs_ref, 0, sem).wait()
                return carry

            lax.fori_loop(0, fill_n_ref[g], start_fill, 0)
            lax.fori_loop(0, fill_n_ref[g], wait_fill, 0)


def _scatter_call(dest, fill_start, fill_n, x1t, p_rows):
    n = x1t.shape[0]
    tm = TM_PERM
    grid_spec = pltpu.PrefetchScalarGridSpec(
        num_scalar_prefetch=3,
        grid=(n // tm,),
        in_specs=[pl.BlockSpec((tm, SUBLANES, LANES), lambda i, *_: (i, 0, 0))],
        out_specs=pl.BlockSpec(memory_space=pl.ANY),
        scratch_shapes=[pltpu.VMEM((1, SUBLANES, LANES), F32), pltpu.SemaphoreType.DMA(())],
    )
    return pl.pallas_call(
        _scatter_kernel,
        grid_spec=grid_spec,
        out_shape=jax.ShapeDtypeStruct((p_rows, SUBLANES, LANES), F32),
        compiler_params=pltpu.CompilerParams(dimension_semantics=("arbitrary",),
                                             vmem_limit_bytes=VMEM_LIMIT),
        name="row_scatter",
    )(dest, fill_start, fill_n, x1t)


def _moe_kernel(tg_ref, tv_ref, xs_ref, g2_ref, wrh_ref, wrl_ref, br_ref, wg_ref, wu_ref, wd_ref,
                o_ref):
    i = pl.program_id(0)
    t = xs_ref.shape[0]

    @pl.when(tv_ref[i] == 0)
    def _():
        o_ref[...] = jnp.zeros_like(o_ref)

    @pl.when(tv_ref[i] > 0)
    def _():
        x1 = xs_ref[...].reshape(t, D_MODEL)
        h2 = _ffn_norm(x1, g2_ref)

        rowi = lax.broadcasted_iota(jnp.int32, (SUBLANES, t), 0).astype(F32)
        neg = jnp.float32(-jnp.inf)
        logits = _router_logits(wrh_ref, wrl_ref, br_ref, h2)
        gl = jnp.where(rowi < N_GROUPS, logits[0:SUBLANES, :], neg)
        mine = jnp.sum(jnp.where(rowi == tg_ref[i].astype(F32), gl, 0.0), axis=0, keepdims=True)
        g_w = 1.0 / jnp.sum(jnp.exp(gl - mine), axis=0, keepdims=True)
        el = logits[SUBLANES:2 * SUBLANES, :]
        t1, i1 = _first_argmax(el, rowi)
        t2, i2 = _first_argmax(jnp.where(rowi == i1, neg, el), rowi)
        e = jnp.exp(t2 - t1)
        w1 = g_w / (1.0 + e)
        w2 = g_w * e / (1.0 + e)
        cw_t = jnp.where(rowi == i1, w1, 0.0) + jnp.where(rowi == i2, w2, 0.0)
        cw = jnp.concatenate([cw_t, jnp.zeros((LANES - SUBLANES, t), F32)], axis=0).T

        hb = h2.astype(BF16)
        gate_up = lambda ex: (_dot(hb, wg_ref[ex]), _dot(hb, wu_ref[ex]))
        acc = x1
        ahead = gate_up(0)
        for ex in range(E_PER_GROUP):
            gate, up = ahead
            if ex + 1 < E_PER_GROUP:
                ahead = gate_up(ex + 1)
            act = (gate * jax.nn.sigmoid(gate)) * up * cw[:, ex:ex + 1]
            acc = acc + _dot(act.astype(BF16), wd_ref[ex])
        o_ref[...] = acc.reshape(t, SUBLANES, LANES)


def _moe_call(tile_group, tile_valid, xs, g2, wrh, wrl, br, wg, wu, wd):
    p = xs.shape[0]
    t = TM_MOE
    by_block = lambda i, tg, tv: (i, 0, 0)
    by_group = lambda i, tg, tv: (tg[i], 0, 0)
    once = pl.Buffered(1)
    grid_spec = pltpu.PrefetchScalarGridSpec(
        num_scalar_prefetch=2,
        grid=(p // t,),
        in_specs=[pl.BlockSpec((t, SUBLANES, LANES), by_block),
                  pl.BlockSpec((1, D_MODEL), lambda i, *_: (0, 0)),
                  pl.BlockSpec((1, D_MODEL, LANES), by_group, pipeline_mode=once),
                  pl.BlockSpec((1, D_MODEL, LANES), by_group, pipeline_mode=once),
                  pl.BlockSpec((1, 1, LANES), by_group, pipeline_mode=once),
                  pl.BlockSpec((E_PER_GROUP, D_MODEL, D_EXPERT), by_group),
                  pl.BlockSpec((E_PER_GROUP, D_MODEL, D_EXPERT), by_group),
                  pl.BlockSpec((E_PER_GROUP, D_EXPERT, D_MODEL), by_group)],
        out_specs=pl.BlockSpec((t, SUBLANES, LANES), by_block),
    )
    return pl.pallas_call(
        _moe_kernel,
        grid_spec=grid_spec,
        out_shape=jax.ShapeDtypeStruct((p, SUBLANES, LANES), F32),
        compiler_params=pltpu.CompilerParams(dimension_semantics=("arbitrary",),
                                             vmem_limit_bytes=VMEM_LIMIT),
        name="moe_experts",
    )(tile_group, tile_valid, xs, g2, wrh, wrl, br, wg, wu, wd)


def _gather_kernel(dest_ref, ys_ref, o_ref, rows_scr, sem):
    i = pl.program_id(0)
    tm = o_ref.shape[0]
    base = i * tm

    def start(k, carry):
        for u in range(ISSUE_UNROLL):
            r = k * ISSUE_UNROLL + u
            _row_copy(ys_ref, dest_ref[base + r], rows_scr, r, sem).start(priority=u % 2)
        return carry

    lax.fori_loop(0, tm // ISSUE_UNROLL, start, 0)
    pltpu.make_async_copy(ys_ref.at[pl.ds(0, tm)], rows_scr, sem).wait()
    o_ref[...] = rows_scr[...].reshape(tm, D_MODEL)


def _gather_call(dest, ys, n):
    tm = TM_PERM
    grid_spec = pltpu.PrefetchScalarGridSpec(
        num_scalar_prefetch=1,
        grid=(n // tm,),
        in_specs=[pl.BlockSpec(memory_space=pl.ANY)],
        out_specs=pl.BlockSpec((tm, D_MODEL), lambda i, *_: (i, 0)),
        scratch_shapes=[pltpu.VMEM((tm, SUBLANES, LANES), F32), pltpu.SemaphoreType.DMA(())],
    )
    return pl.pallas_call(
        _gather_kernel,
        grid_spec=grid_spec,
        out_shape=jax.ShapeDtypeStruct((n, D_MODEL), F32),
        compiler_params=pltpu.CompilerParams(dimension_semantics=("arbitrary",),
                                             vmem_limit_bytes=VMEM_LIMIT),
        name="row_gather",
    )(dest, ys)


def _router_slabs(w_rg, b_rg, w_re, b_re):
    def slabs(group_part, expert_part):
        lead = expert_part.shape[:-1]
        pads = [jnp.zeros(lead + (width,), F32)
                for width in (SUBLANES - N_GROUPS, LANES - SUBLANES - E_PER_GROUP)]
        return jnp.concatenate([jnp.broadcast_to(group_part, lead + (N_GROUPS,)), pads[0],
                                expert_part, pads[1]], axis=-1)

    w = slabs(w_rg, w_re.reshape(D_MODEL, N_GROUPS, E_PER_GROUP).transpose(1, 0, 2))
    b = slabs(b_rg, b_re.reshape(N_GROUPS, 1, E_PER_GROUP))
    hi = w.astype(BF16)
    lo = (w - hi.astype(F32)).astype(BF16)
    return hi, lo, b


def _layer(x2d, batch, seq_len, norm_mix_g, w_in, q_norm_g, k_norm_g, conv_w, w_sb_branch,
           w_conv_branch, w_out, norm_ffn_g, w_router_group, b_router_group, w_router_expert,
           b_router_expert, w_gate_e, w_up_e, w_down_e):
    n = x2d.shape[0]
    lane = jnp.arange(WIDTH)
    head_mean = jnp.where(lane[:, None] // HEAD_DIM == lane[None, :] // HEAD_DIM,
                          1.0 / HEAD_DIM, 0.0).astype(BF16)
    qg = (jnp.tile(q_norm_g, N_HEADS) * HEAD_DIM ** -0.5)[None, :]
    kg = jnp.tile(k_norm_g, N_HEADS)[None, :]
    qt, k, vt, y, sa, sb = _inproj_call(x2d, norm_mix_g[None, :], w_in.astype(BF16), qg, kg,
                                        head_mean, conv_w, seq_len)

    pos = jnp.arange(TK)
    tri = (pos[None, :] >= pos[:, None]).astype(BF16)
    attn, (wg16, wu16, wd16) = _attn_call(qt, k, vt, tri, (w_gate_e, w_up_e, w_down_e), batch,
                                          seq_len)

    wrh, wrl, br = _router_slabs(w_router_group, b_router_group, w_router_expert, b_router_expert)
    tpos = jnp.arange(TM_PROJ)
    upper = (tpos[:, None] < tpos[None, :]).astype(BF16)
    x1t, route, cnt = _merge_call(attn, y, sa, sb, x2d, w_sb_branch.astype(BF16),
                                  w_conv_branch.astype(BF16), w_out.astype(BF16),
                                  norm_ffn_g[None, :], wrh, wrl, br, upper)

    t = TM_MOE
    grp = route[0].astype(jnp.int32)
    rank = route[1].astype(jnp.int32)
    counts = cnt[0:N_GROUPS, 0].astype(jnp.int32)
    padded = ((counts + t - 1) // t) * t
    ends = jnp.cumsum(padded)
    offs = ends - padded
    dest = rank
    for g in range(N_GROUPS):
        dest = dest + jnp.where(grp == g, offs[g], 0)
    n_tiles = n // t + N_GROUPS
    tile_start = jnp.arange(n_tiles, dtype=jnp.int32) * t
    tile_valid = (tile_start < ends[-1]).astype(jnp.int32)
    tile_group = jnp.zeros_like(tile_start)
    for g in range(N_GROUPS - 1):
        tile_group = tile_group + (jnp.minimum(tile_start, ends[-1] - t) >= ends[g]).astype(jnp.int32)
    p_rows = n_tiles * t
    fill_start = jnp.concatenate([offs + counts, ends[-1:]])
    fill_n = jnp.concatenate([padded - counts, p_rows - ends[-1:]])

    xs = _scatter_call(dest, fill_start, fill_n, x1t, p_rows)
    ys = _moe_call(tile_group, tile_valid, xs, norm_ffn_g[None, :], wrh, wrl, br, wg16, wu16, wd16)
    return _gather_call(dest, ys, n)


def kernel(x, norm_mix_g, w_in, q_norm_g, k_norm_g, conv_w, w_sb_branch, w_conv_branch, w_out,
           norm_ffn_g, w_router_group, b_router_group, w_router_expert, b_router_expert,
           w_gate_e, w_up_e, w_down_e):
    batch, seq_len, d = x.shape
    x2d = x.reshape(batch * seq_len, d)
    for l in range(norm_mix_g.shape[0]):
        x2d = _layer(x2d, batch, seq_len, norm_mix_g[l], w_in[l], q_norm_g[l], k_norm_g[l],
                     conv_w[l], w_sb_branch[l], w_conv_branch[l], w_out[l], norm_ffn_g[l],
                     w_router_group[l], b_router_group[l], w_router_expert[l], b_router_expert[l],
                     w_gate_e[l], w_up_e[l], w_down_e[l])
    return x2d.reshape(batch, seq_len, d)
```
